```python
import jax, jax.numpy as jnp
from jax import lax
import numpy as np

D_MODEL = 2048
BATCH = 1
SEQ = 16384
DEPTH = 1

GRID_W = 64
CTX_LEN = 256
D_CONV = 2048
CONV_WIDTH = 31
D_RNN = 2048
RNN_BLOCKS = 16
RNN_BW = D_RNN // RNN_BLOCKS
SHORT_CONV = 4
LRU_C = 8.0
EPS = 1e-6

SPLITS = [D_CONV, 2 * D_CONV, 3 * D_CONV, 3 * D_CONV + D_RNN, 3 * D_CONV + 2 * D_RNN,
          3 * D_CONV + 2 * D_RNN + D_MODEL]
D_IN = 3 * D_CONV + 2 * D_RNN + 2 * D_MODEL
RX0 = 3 * D_CONV
RX1 = RX0 + D_RNN

kernel_name = "hybrid_conformer_rglru_dit_block"


def rmsnorm(x, g):
    xf = x.astype(jnp.float32)
    y = xf * lax.rsqrt(jnp.mean(xf * xf, axis=-1, keepdims=True) + EPS)
    return (y * g.astype(jnp.float32)).astype(x.dtype)


def layernorm(x, g, b):
    xf = x.astype(jnp.float32)
    mu = jnp.mean(xf, axis=-1, keepdims=True)
    var = jnp.mean(jnp.square(xf - mu), axis=-1, keepdims=True)
    y = (xf - mu) * lax.rsqrt(var + EPS)
    return (y * g.astype(jnp.float32) + b.astype(jnp.float32)).astype(x.dtype)


def adaln(c_vec, w_ada, b_ada):
    m = jax.nn.silu(c_vec) @ w_ada + b_ada
    return jnp.split(m, 3, axis=-1)


def depthwise_conv(v, k, pad_lo, pad_hi):
    return lax.conv_general_dilated(
        v, k[:, None, :].astype(v.dtype), window_strides=(1,), padding=[(pad_lo, pad_hi)],
        dimension_numbers=("NWC", "WIO", "NWC"), feature_group_count=v.shape[-1])


def conformer_branch(a, g, z, p, grid):
    v = a * jax.nn.sigmoid(g)
    bsz, length, ch = v.shape
    half = CONV_WIDTH // 2
    if grid:
        rows = length // GRID_W
        vr = v.reshape(bsz * rows, GRID_W, ch)
        v = depthwise_conv(vr, p["conv_dw"], half, half).reshape(bsz, length, ch)
    else:
        v = depthwise_conv(v, p["conv_dw"], half, half)
    v = v + p["conv_dw_b"]
    v = jax.nn.silu(layernorm(v, p["conv_ln_g"], p["conv_ln_b"]))
    return (v * jax.nn.silu(z)) @ p["w_conv_out"]


def _lru_combine(left, right):
    a_l, b_l = left
    a_r, b_r = right
    return a_l * a_r, a_r * b_l + b_r


def rglru_direction(xr, p, d, h0, reverse):
    pad = (0, SHORT_CONV - 1) if reverse else (SHORT_CONV - 1, 0)
    xc = depthwise_conv(xr, p["rnn_conv"][d], pad[0], pad[1]) + p["rnn_conv_b"][d]
    bsz, length, _ = xc.shape
    xb = xc.reshape(bsz, length, RNN_BLOCKS, RNN_BW)
    r = jax.nn.sigmoid((jnp.einsum("blhi,hij->blhj", xb, p["rnn_w_r"][d]).reshape(bsz, length, D_RNN)
                        + p["rnn_b_r"][d]).astype(jnp.float32))
    i = jax.nn.sigmoid((jnp.einsum("blhi,hij->blhj", xb, p["rnn_w_i"][d]).reshape(bsz, length, D_RNN)
                        + p["rnn_b_i"][d]).astype(jnp.float32))
    log_a = -LRU_C * r * jax.nn.softplus(-p["rnn_lam"][d].astype(jnp.float32))
    a = jnp.exp(log_a)
    b = jnp.sqrt(-jnp.expm1(2.0 * log_a)) * i * xc.astype(jnp.float32)
    a_cum, b_cum = lax.associative_scan(_lru_combine, (a, b), axis=1, reverse=reverse)
    return b_cum + a_cum * h0[:, None, :]


def mixer(u, p, grid, h0f, h0b):
    a, g, z_c, xr, z_r, gc, gr = jnp.split(u, SPLITS, axis=-1)
    y_conv = conformer_branch(a, g, z_c, p, grid)
    hf = rglru_direction(xr, p, 0, h0f, False)
    hb = rglru_direction(xr, p, 1, h0b, True)
    y_rnn = ((hf + hb).astype(u.dtype) * jax.nn.silu(z_r)) @ p["w_rnn_out"]
    y = jax.nn.sigmoid(gc) * y_conv + jax.nn.sigmoid(gr) * y_rnn
    return y @ p["w_o"], hf[:, -1], hb[:, 0]


def setup_inputs(seed: int = 0) -> dict:
    key = jax.random.key(seed)
    ks = jax.random.split(key, 24)
    f32 = jnp.float32
    nrm = lambda k, shape, s: jax.random.normal(k, shape, f32) * s
    a0 = jax.random.uniform(ks[20], (DEPTH, 2, D_RNN), f32, 0.9, 0.999)
    s0 = a0 ** (1.0 / LRU_C)
    return {
        "x": nrm(ks[0], (BATCH, SEQ, D_MODEL), 1.0),
        "c": nrm(ks[1], (BATCH, D_MODEL), 1.0),
        "ctx": nrm(ks[2], (BATCH, CTX_LEN, D_MODEL), 1.0),
        "c_ctx": nrm(ks[3], (D_MODEL,), 1.0),
        "w_ada": nrm(ks[4], (DEPTH, D_MODEL, 3 * D_MODEL), 0.5 * D_MODEL ** -0.5),
        "b_ada": nrm(ks[5], (DEPTH, 3 * D_MODEL), 0.01),
        "norm_g": 1.0 + nrm(ks[6], (DEPTH, D_MODEL), 0.05),
        "w_in": nrm(ks[7], (DEPTH, D_MODEL, D_IN), D_MODEL ** -0.5),
        "b_in": nrm(ks[8], (DEPTH, D_IN), 0.01),
        "conv_dw": nrm(ks[9], (DEPTH, CONV_WIDTH, D_CONV), CONV_WIDTH ** -0.5),
        "conv_dw_b": nrm(ks[10], (DEPTH, D_CONV), 0.01),
        "conv_ln_g": 1.0 + nrm(ks[11], (DEPTH, D_CONV), 0.05),
        "conv_ln_b": nrm(ks[12], (DEPTH, D_CONV), 0.01),
        "w_conv_out": nrm(ks[13], (DEPTH, D_CONV, D_MODEL), D_CONV ** -0.5),
        "rnn_conv": nrm(ks[14], (DEPTH, 2, SHORT_CONV, D_RNN), SHORT_CONV ** -0.5),
        "rnn_conv_b": nrm(ks[15], (DEPTH, 2, D_RNN), 0.01),
        "rnn_w_r": nrm(ks[16], (DEPTH, 2, RNN_BLOCKS, RNN_BW, RNN_BW), RNN_BW ** -0.5),
        "rnn_b_r": nrm(ks[17], (DEPTH, 2, D_RNN), 0.01),
        "rnn_w_i": nrm(ks[18], (DEPTH, 2, RNN_BLOCKS, RNN_BW, RNN_BW), RNN_BW ** -0.5),
        "rnn_b_i": nrm(ks[19], (DEPTH, 2, D_RNN), 0.01),
        "rnn_lam": jnp.log(s0) - jnp.log1p(-s0),
        "w_rnn_out": nrm(ks[21], (DEPTH, D_RNN, D_MODEL), D_RNN ** -0.5),
        "w_o": nrm(ks[22], (DEPTH, D_MODEL, D_MODEL), D_MODEL ** -0.5),
        "final_g": 1.0 + nrm(ks[23], (D_MODEL,), 0.05),
    }


def reference(x, c, ctx, c_ctx, w_ada, b_ada, norm_g, w_in, b_in, conv_dw, conv_dw_b, conv_ln_g,
              conv_ln_b, w_conv_out, rnn_conv, rnn_conv_b, rnn_w_r, rnn_b_r, rnn_w_i, rnn_b_i,
              rnn_lam, w_rnn_out, w_o, final_g):
    bsz = x.shape[0]
    for l in range(DEPTH):
        p = {
            "conv_dw": conv_dw[l], "conv_dw_b": conv_dw_b[l], "conv_ln_g": conv_ln_g[l],
            "conv_ln_b": conv_ln_b[l], "w_conv_out": w_conv_out[l], "rnn_conv": rnn_conv[l],
            "rnn_conv_b": rnn_conv_b[l], "rnn_w_r": rnn_w_r[l], "rnn_b_r": rnn_b_r[l],
            "rnn_w_i": rnn_w_i[l], "rnn_b_i": rnn_b_i[l], "rnn_lam": rnn_lam[l],
            "w_rnn_out": w_rnn_out[l], "w_o": w_o[l],
        }
        sh, sc, gt = adaln(c, w_ada[l], b_ada[l])
        sh_c, sc_c, gt_c = adaln(c_ctx, w_ada[l], b_ada[l])
        hn = rmsnorm(x, norm_g[l]) * (1.0 + sc[:, None]) + sh[:, None]
        hn_ctx = rmsnorm(ctx, norm_g[l]) * (1.0 + sc_c) + sh_c
        zeros = jnp.zeros((bsz, D_RNN), jnp.float32)
        if l < DEPTH - 1:
            u_ctx = hn_ctx @ w_in[l] + b_in[l]
            y_ctx, hcf, hcb = mixer(u_ctx, p, False, zeros, zeros)
            ctx = ctx + gt_c * y_ctx
        else:
            xr_ctx = hn_ctx @ w_in[l, :, RX0:RX1] + b_in[l, RX0:RX1]
            hcf = rglru_direction(xr_ctx, p, 0, zeros, False)[:, -1]
            hcb = rglru_direction(xr_ctx, p, 1, zeros, True)[:, 0]
        u = hn @ w_in[l] + b_in[l]
        y, _, _ = mixer(u, p, True, hcf, hcb)
        x = x + gt[:, None] * y
    return rmsnorm(x, final_g)
```

```python
import functools
import math

import jax
import jax.numpy as jnp
from jax import lax
from jax.experimental import pallas as pl
from jax.experimental.pallas import tpu as pltpu

EPS = 1e-6
LRU_C = 8.0
GRID_W = 64
CONV_W = 31
CONV_HALF = CONV_W // 2
CONV_PAD = 16
SHORT = 4
RNN_BW = 128

SUBLANES = 8
LANES = 128
VMEM_LIMIT_BYTES = 56 * 1024 * 1024

F32 = jnp.float32
BF16 = jnp.bfloat16


def _sigmoid(x):
    return jax.nn.sigmoid(x)


def _silu(x):
    return x * jax.nn.sigmoid(x)


def _rms_mod(x, g, scale, shift):
    ms = jnp.mean(x * x, axis=-1, keepdims=True)
    y = x * lax.rsqrt(ms + EPS)
    return y * (g * (1.0 + scale)) + shift


def _adaln_kernel(cc_ref, w_ref, b_ref, o_ref):
    s = _silu(cc_ref[...])
    o_ref[...] = jnp.dot(s, w_ref[...], preferred_element_type=F32,
                         precision=lax.Precision.HIGHEST) + b_ref[...]


def _adaln(cc, w, b):
    d, n = w.shape
    bn = min(d, 1024)
    assert n % bn == 0
    return pl.pallas_call(
        _adaln_kernel,
        grid=(n // bn,),
        in_specs=[pl.BlockSpec((SUBLANES, d), lambda j: (0, 0)),
                  pl.BlockSpec((d, bn), lambda j: (0, j)),
                  pl.BlockSpec((1, bn), lambda j: (0, j))],
        out_specs=pl.BlockSpec((SUBLANES, bn), lambda j: (0, j)),
        out_shape=jax.ShapeDtypeStruct((SUBLANES, n), F32),
        compiler_params=pltpu.CompilerParams(dimension_semantics=("arbitrary",),
                                             vmem_limit_bytes=VMEM_LIMIT_BYTES),
        name="adaln",
    )(cc, w, b)


def _halo(xv, nbr, reverse, sub):
    p = xv.shape[0]
    out = []
    for j in range(SHORT - 1):
        if reverse:
            own = pltpu.roll(xv[j], SUBLANES - 1, axis=0)
            oth = pltpu.roll(nbr[j], SUBLANES - 1, axis=0)
            out.append(jnp.where(sub == SUBLANES - 1, oth, own))
        else:
            own = pltpu.roll(xv[p - (SHORT - 1) + j], 1, axis=0)
            oth = pltpu.roll(nbr[j], 1, axis=0)
            out.append(jnp.where(sub == 0, oth, own))
    return jnp.stack(out, axis=0)


def _short_conv(xv, halo, cw_ref, cb, reverse):
    p = xv.shape[0]
    xe = jnp.concatenate([xv, halo] if reverse else [halo, xv], axis=0)
    acc = cb[None] + cw_ref[0][None] * xe[0:p]
    for k in range(1, SHORT):
        acc = acc + cw_ref[k][None] * xe[k:k + p]
    return acc


def _gates_ab(xc, wri_ref, br, bi, lam, a_ref, b_ref):
    p, _, c = xc.shape
    x2 = xc.reshape(p * SUBLANES, c)
    xb = x2.astype(BF16)
    neg = -lam
    softplus = jnp.maximum(neg, 0.0) + jnp.log(1.0 + jnp.exp(-jnp.abs(neg)))
    cl2 = softplus * (-LRU_C * math.log2(math.e))
    for h in range(c // RNN_BW):
        sl = slice(h * RNN_BW, (h + 1) * RNN_BW)
        pre = jnp.dot(xb[:, sl], wri_ref[h], preferred_element_type=F32)
        r = _sigmoid(pre[:, :RNN_BW] + br[:, sl])
        i = _sigmoid(pre[:, RNN_BW:] + bi[:, sl])
        a = jnp.exp2(r * cl2[:, sl])
        om = 1.0 - a * a
        mult = jnp.where(om > 0.0, om * lax.rsqrt(om), 0.0)
        b = mult * (i * x2[:, sl])
        a_ref[:, :, sl] = a.reshape(p, SUBLANES, RNN_BW)
        b_ref[:, :, sl] = b.reshape(p, SUBLANES, RNN_BW)


def _lru_scan(a_ref, b_ref, carry_prev, reverse, h_ref):
    p, _, c = a_ref.shape
    sub = lax.broadcasted_iota(jnp.int32, (SUBLANES, c), 0)

    def col(t):
        return (p - 1 - t) if reverse else t

    def local(t, hc):
        h, acc = hc
        a = a_ref[col(t)]
        return a * h + b_ref[col(t)], acc * a

    unroll = 8 if p % 8 == 0 else 1
    b_end, a_end = lax.fori_loop(0, p, local, (jnp.zeros((SUBLANES, c), F32),
                                               jnp.ones((SUBLANES, c), F32)), unroll=unroll)
    for s in (1, 2, 4):
        shift = (SUBLANES - s) if reverse else s
        a_sh = pltpu.roll(a_end, shift, axis=0)
        b_sh = pltpu.roll(b_end, shift, axis=0)
        valid = (sub < SUBLANES - s) if reverse else (sub >= s)
        b_end = jnp.where(valid, a_end * b_sh + b_end, b_end)
        a_end = jnp.where(valid, a_end * a_sh, a_end)
    last = 0 if reverse else SUBLANES - 1
    first = SUBLANES - 1 - last
    c_in = jnp.broadcast_to(carry_prev[last:last + 1, :], (SUBLANES, c))
    h_out = b_end + a_end * c_in
    if h_ref is not None:
        h_in = jnp.where(sub == first, c_in,
                         pltpu.roll(h_out, (SUBLANES - 1) if reverse else 1, axis=0))

        def final(t, h):
            h = a_ref[col(t)] * h + b_ref[col(t)]
            h_ref[col(t)] = h
            return h

        lax.fori_loop(0, p, final, h_in, unroll=unroll)
    return h_out


def _rnn_chunk(xv, j, i, h0_ref, nbr_s, carry_s, cw_ref, cb_ref, wri_ref, br_ref, bi_ref, lam_ref,
               a_s, b_s, reverse, h_ref):
    p, _, c = xv.shape
    sub = lax.broadcasted_iota(jnp.int32, (SUBLANES, c), 0)

    @pl.when(i == 0)
    def _():
        nbr_s[j] = jnp.zeros(nbr_s.shape[1:], F32)
        carry_s[j] = h0_ref[...]

    halo = _halo(xv, nbr_s[j], reverse, sub)
    nbr_s[j] = xv[0:SHORT - 1] if reverse else xv[p - (SHORT - 1):p]
    xc = _short_conv(xv, halo, cw_ref, cb_ref[...], reverse)
    _gates_ab(xc, wri_ref, br_ref[...], bi_ref[...], lam_ref[...], a_s, b_s)
    h_out = _lru_scan(a_s, b_s, carry_s[j], reverse, h_ref)
    carry_s[j] = h_out
    return h_out


def _rnn_pass_kernel(x_ref, shift_ref, scale_ref, g_ref, w_ref, bx_ref, cw_ref, cb_ref, wri_ref,
                     br_ref, bi_ref, lam_ref, h0_ref, *rest, reverse, mod_row, write_states):
    if write_states:
        xr_ref, h_ref, hn_s, a_s, b_s, nbr_s, carry_s = rest
        hfin_ref = None
    else:
        hfin_ref, hn_s, a_s, b_s, nbr_s, carry_s = rest
        xr_ref = h_ref = None
    i = pl.program_id(0)
    j = pl.program_id(1)
    p, _, d = x_ref.shape
    c = w_ref.shape[1]

    @pl.when(j == 0)
    def _():
        x = x_ref[...].reshape(p * SUBLANES, d)
        row = slice(mod_row, mod_row + 1)
        hn_s[...] = _rms_mod(x, g_ref[...], scale_ref[row, :], shift_ref[row, :]).astype(BF16)

    xr = jnp.dot(hn_s[...], w_ref[...], preferred_element_type=F32) + bx_ref[...]
    xv = xr.reshape(p, SUBLANES, c)
    if write_states:
        xr_ref[...] = xv
    h_end = _rnn_chunk(xv, j, i, h0_ref, nbr_s, carry_s, cw_ref, cb_ref, wri_ref, br_ref, bi_ref,
                       lam_ref, a_s, b_s, reverse, h_ref)
    if not write_states:
        hfin_ref[...] = h_end


def _rnn_pass(x3, mod, norm_g, w_in, b_in, xr_col, cw, cb, wri, br, bi, lam, h0, *, p, c, reverse,
              mod_row, write_states):
    s8, _, d = x3.shape
    dr = cw.shape[-1]
    nt = s8 // p
    nc = dr // c
    t = p * SUBLANES

    def tile(i):
        return (nt - 1 - i) if reverse else i

    in_specs = [
        pl.BlockSpec((p, SUBLANES, d), lambda i, j: (tile(i), 0, 0)),
        pl.BlockSpec((SUBLANES, d), lambda i, j: (0, 0)),
        pl.BlockSpec((SUBLANES, d), lambda i, j: (0, 1)),
        pl.BlockSpec((1, d), lambda i, j: (0, 0)),
        pl.BlockSpec((d, c), lambda i, j: (0, xr_col * nc + j)),
        pl.BlockSpec((1, c), lambda i, j: (0, xr_col * nc + j)),
        pl.BlockSpec((SHORT, SUBLANES, c), lambda i, j: (0, 0, j)),
        pl.BlockSpec((SUBLANES, c), lambda i, j: (0, j)),
        pl.BlockSpec((c // RNN_BW, RNN_BW, 2 * RNN_BW), lambda i, j: (j, 0, 0)),
        pl.BlockSpec((1, c), lambda i, j: (0, j)),
        pl.BlockSpec((1, c), lambda i, j: (0, j)),
        pl.BlockSpec((1, c), lambda i, j: (0, j)),
        pl.BlockSpec((SUBLANES, c), lambda i, j: (0, j)),
    ]
    if write_states:
        st_spec = pl.BlockSpec((p, SUBLANES, c), lambda i, j: (tile(i), 0, j))
        st_shape = jax.ShapeDtypeStruct((s8, SUBLANES, dr), F32)
        out_specs = [st_spec, st_spec]
        out_shape = [st_shape, st_shape]
    else:
        assert nt == 1
        out_specs = pl.BlockSpec((SUBLANES, c), lambda i, j: (0, j))
        out_shape = jax.ShapeDtypeStruct((SUBLANES, dr), F32)
    scratch = [
        pltpu.VMEM((t, d), BF16),
        pltpu.VMEM((p, SUBLANES, c), F32),
        pltpu.VMEM((p, SUBLANES, c), F32),
        pltpu.VMEM((nc, SHORT - 1, SUBLANES, c), F32),
        pltpu.VMEM((nc, SUBLANES, c), F32),
    ]
    kern = functools.partial(_rnn_pass_kernel, reverse=reverse, mod_row=mod_row,
                             write_states=write_states)
    return pl.pallas_call(
        kern,
        grid=(nt, nc),
        in_specs=in_specs,
        out_specs=out_specs,
        out_shape=out_shape,
        scratch_shapes=scratch,
        compiler_params=pltpu.CompilerParams(dimension_semantics=("arbitrary", "arbitrary"),
                                             vmem_limit_bytes=VMEM_LIMIT_BYTES),
        name="rnn_rev" if reverse else "rnn_fwd",
    )(x3, mod, mod, norm_g, w_in, b_in, cw, cb, wri, br, bi, lam, h0)


def _mixer_kernel(x_ref, shift_ref, scale_ref, g_ref,
                  wa_ref, wg_ref, wzc_ref, wzr_ref, wgc_ref, wgr_ref,
                  ba_ref, bg_ref, bzc_ref, bzr_ref, bgc_ref, bgr_ref,
                  dw_ref, dwb_ref, lng_ref, lnb_ref,
                  xr_ref, hb_ref, cw_ref, cb_ref, wri_ref, br_ref, bi_ref, lam_ref, h0_ref,
                  lc_ref, lr_ref, sgc_ref, sgr_ref,
                  hn_s, vpad_s, cv_s, szc_s, a_s, b_s, hf_s, nbr_s, carry_s):
    i = pl.program_id(0)
    j = pl.program_id(1)
    nc = pl.num_programs(1)
    p, _, d = x_ref.shape
    c = wa_ref.shape[1]
    t = p * SUBLANES

    @pl.when(j == 0)
    def _():
        x = x_ref[...].reshape(t, d)
        hn_s[...] = _rms_mod(x, g_ref[...], scale_ref[0:1, :], shift_ref[0:1, :]).astype(BF16)

    @pl.when((i == 0) & (j == 0))
    def _():
        zeros = jnp.zeros((CONV_PAD, SUBLANES, c), F32)
        vpad_s[0:CONV_PAD] = zeros
        vpad_s[CONV_PAD + p:CONV_PAD + p + CONV_PAD] = zeros

    hn = hn_s[...]

    def proj(w_ref, b_ref):
        return jnp.dot(hn, w_ref[...], preferred_element_type=F32) + b_ref[...]

    v = proj(wa_ref, ba_ref) * _sigmoid(proj(wg_ref, bg_ref))
    vpad_s[CONV_PAD:CONV_PAD + p] = v.reshape(p, SUBLANES, c)
    pb = 8
    off = CONV_PAD - CONV_HALF
    for q in range(p // pb):
        for lb in range(c // LANES):
            sl = slice(lb * LANES, (lb + 1) * LANES)
            acc = jnp.broadcast_to(dwb_ref[:, sl][None], (pb, SUBLANES, LANES))
            for k in range(CONV_W):
                acc = acc + dw_ref[k, :, sl][None] * vpad_s[q * pb + k + off:q * pb + k + off + pb, :, sl]
            cv_s[j, q * pb:(q + 1) * pb, :, sl] = acc
    szc_s[j] = _silu(proj(wzc_ref, bzc_ref))

    _rnn_chunk(xr_ref[...], j, i, h0_ref, nbr_s, carry_s, cw_ref, cb_ref, wri_ref, br_ref, bi_ref,
               lam_ref, a_s, b_s, False, hf_s)
    szr = _silu(proj(wzr_ref, bzr_ref))
    h = (hf_s[...] + hb_ref[...]).reshape(t, c)
    lr_ref[...] = (h * szr).astype(BF16)

    sgc_ref[...] = _sigmoid(proj(wgc_ref, bgc_ref))
    sgr_ref[...] = _sigmoid(proj(wgr_ref, bgr_ref))

    @pl.when(j == nc - 1)
    def _():
        n_chunks = cv_s.shape[0]
        tot = jnp.zeros((t, 1), F32)
        for cc in range(n_chunks):
            tot = tot + jnp.sum(cv_s[cc].reshape(t, c), axis=-1, keepdims=True)
        mu = tot * (1.0 / d)
        tot = jnp.zeros((t, 1), F32)
        for cc in range(n_chunks):
            dev = cv_s[cc].reshape(t, c) - mu
            tot = tot + jnp.sum(dev * dev, axis=-1, keepdims=True)
        rstd = lax.rsqrt(tot * (1.0 / d) + EPS)
        for cc in range(n_chunks):
            sl = slice(cc * c, (cc + 1) * c)
            y = (cv_s[cc].reshape(t, c) - mu) * rstd * lng_ref[:, sl] + lnb_ref[:, sl]
            lc_ref[:, sl] = (_silu(y) * szc_s[cc]).astype(BF16)


def _mixer(x3, mod, norm_g, w_in, b_in, dw, dwb, lng, lnb, xr, hb, cw, cb, wri, br, bi, lam, h0,
           *, p, c):
    s8, _, d = x3.shape
    nt = s8 // p
    nc = d // c
    t = p * SUBLANES
    s = s8 * SUBLANES

    def wspec(piece):
        return pl.BlockSpec((d, c), lambda i, j: (0, piece * nc + j))

    def bspec(piece):
        return pl.BlockSpec((1, c), lambda i, j: (0, piece * nc + j))

    def row(width):
        return pl.BlockSpec((1, width), lambda i, j: (0, 0))

    chunk_row = pl.BlockSpec((1, c), lambda i, j: (0, j))
    chunk_8 = pl.BlockSpec((SUBLANES, c), lambda i, j: (0, j))
    tok3 = pl.BlockSpec((p, SUBLANES, c), lambda i, j: (i, 0, j))
    tok2 = pl.BlockSpec((t, c), lambda i, j: (i, j))
    pieces = (0, 1, 2, 4, 5, 6)
    in_specs = (
        [pl.BlockSpec((p, SUBLANES, d), lambda i, j: (i, 0, 0)),
         pl.BlockSpec((SUBLANES, d), lambda i, j: (0, 0)),
         pl.BlockSpec((SUBLANES, d), lambda i, j: (0, 1)),
         row(d)]
        + [wspec(k) for k in pieces]
        + [bspec(k) for k in pieces]
        + [pl.BlockSpec((CONV_W, SUBLANES, c), lambda i, j: (0, 0, j)), chunk_row, row(d), row(d),
           tok3, tok3,
           pl.BlockSpec((SHORT, SUBLANES, c), lambda i, j: (0, 0, j)), chunk_8,
           pl.BlockSpec((c // RNN_BW, RNN_BW, 2 * RNN_BW), lambda i, j: (j, 0, 0)),
           chunk_row, chunk_row, chunk_row, chunk_8])
    out_specs = [pl.BlockSpec((t, d), lambda i, j: (i, 0)), tok2, tok2, tok2]
    out_shape = [jax.ShapeDtypeStruct((s, d), BF16), jax.ShapeDtypeStruct((s, d), BF16),
                 jax.ShapeDtypeStruct((s, d), F32), jax.ShapeDtypeStruct((s, d), F32)]
    scratch = [
        pltpu.VMEM((t, d), BF16),
        pltpu.VMEM((p + 2 * CONV_PAD, SUBLANES, c), F32),
        pltpu.VMEM((nc, p, SUBLANES, c), F32),
        pltpu.VMEM((nc, t, c), F32),
        pltpu.VMEM((p, SUBLANES, c), F32),
        pltpu.VMEM((p, SUBLANES, c), F32),
        pltpu.VMEM((p, SUBLANES, c), F32),
        pltpu.VMEM((nc, SHORT - 1, SUBLANES, c), F32),
        pltpu.VMEM((nc, SUBLANES, c), F32),
    ]
    args = ([x3, mod, mod, norm_g] + [w_in] * 6 + [b_in] * 6
            + [dw, dwb, lng, lnb, xr, hb, cw, cb, wri, br, bi, lam, h0])
    return pl.pallas_call(
        _mixer_kernel,
        grid=(nt, nc),
        in_specs=in_specs,
        out_specs=out_specs,
        out_shape=out_shape,
        scratch_shapes=scratch,
        compiler_params=pltpu.CompilerParams(dimension_semantics=("arbitrary", "arbitrary"),
                                             vmem_limit_bytes=VMEM_LIMIT_BYTES),
        name="mixer",
    )(*args)


def _out_kernel(lc_ref, lr_ref, sgc_ref, sgr_ref, x_ref, gate_ref, fg_ref, wc_ref, wr_ref, wo_ref,
                o_ref):
    yc = jnp.dot(lc_ref[...], wc_ref[...], preferred_element_type=F32)
    yr = jnp.dot(lr_ref[...], wr_ref[...], preferred_element_type=F32)
    y = (sgc_ref[...] * yc + sgr_ref[...] * yr).astype(BF16)
    o = jnp.dot(y, wo_ref[...], preferred_element_type=F32)
    xn = x_ref[...] + gate_ref[0:1, :] * o
    ms = jnp.mean(xn * xn, axis=-1, keepdims=True)
    o_ref[...] = xn * lax.rsqrt(ms + EPS) * fg_ref[...]


def _out_proj(lc, lr, sgc, sgr, x2, mod, final_g, wc, wr, wo, *, t):
    s, d = x2.shape
    tok = pl.BlockSpec((t, d), lambda i: (i, 0))
    resident = pl.BlockSpec((d, d), lambda i: (0, 0), pipeline_mode=pl.Buffered(1))
    return pl.pallas_call(
        _out_kernel,
        grid=(s // t,),
        in_specs=[tok, tok, tok, tok, tok,
                  pl.BlockSpec((SUBLANES, d), lambda i: (0, 2)),
                  pl.BlockSpec((1, d), lambda i: (0, 0)),
                  resident, resident, resident],
        out_specs=tok,
        out_shape=jax.ShapeDtypeStruct((s, d), F32),
        compiler_params=pltpu.CompilerParams(dimension_semantics=("arbitrary",),
                                             vmem_limit_bytes=VMEM_LIMIT_BYTES),
        name="out_proj",
    )(lc, lr, sgc, sgr, x2, mod, final_g, wc, wr, wo)


def _permute_tokens(x2, p):
    s, d = x2.shape
    nt = s // (p * SUBLANES)
    return x2.reshape(nt, SUBLANES, p, d).transpose(0, 2, 1, 3).reshape(s // SUBLANES, SUBLANES, d)


def _unpermute_tokens(y2, p):
    s, d = y2.shape
    nt = s // (p * SUBLANES)
    return y2.reshape(nt, p, SUBLANES, d).transpose(0, 2, 1, 3).reshape(s, d)


def kernel(x, c, ctx, c_ctx, w_ada, b_ada, norm_g, w_in, b_in, conv_dw, conv_dw_b, conv_ln_g, conv_ln_b, w_conv_out, rnn_conv, rnn_conv_b, rnn_w_r, rnn_b_r, rnn_w_i, rnn_b_i, rnn_lam, w_rnn_out, w_o, final_g):
    bsz, seq, d = x.shape
    ctx_len = ctx.shape[1]
    depth = w_in.shape[0]
    assert bsz == 1 and depth == 1
    assert w_in.shape[2] == 7 * d and rnn_w_r.shape[3] == RNN_BW and conv_dw.shape[1] == CONV_W
    assert seq % (SUBLANES * GRID_W) == 0 and ctx_len % SUBLANES == 0 and d % LANES == 0
    l = 0
    chunk = min(d, 256)
    p_ctx = ctx_len // SUBLANES

    w_in_b = w_in[l].astype(BF16)
    b_in2 = b_in[l][None]
    wri = jnp.concatenate([rnn_w_r[l], rnn_w_i[l]], axis=-1).astype(BF16)
    cw = jnp.broadcast_to(rnn_conv[l][:, :, None, :], (2, SHORT, SUBLANES, d))
    cb = jnp.broadcast_to(rnn_conv_b[l][:, None, :], (2, SUBLANES, d))
    br = rnn_b_r[l][:, None, :]
    bi = rnn_b_i[l][:, None, :]
    lam = rnn_lam[l][:, None, :]
    dw = jnp.broadcast_to(conv_dw[l][:, None, :], (CONV_W, SUBLANES, d))
    norm_g2 = norm_g[l][None]

    cc = jnp.zeros((SUBLANES, d), F32).at[0].set(c[0]).at[1].set(c_ctx)
    mod = _adaln(cc, w_ada[l], b_ada[l][None])

    def rnn_args(direction):
        return (cw[direction], cb[direction], wri[direction], br[direction], bi[direction],
                lam[direction])

    ctx3 = _permute_tokens(ctx[0], p_ctx)
    zeros8 = jnp.zeros((SUBLANES, d), F32)
    h0 = [_rnn_pass(ctx3, mod, norm_g2, w_in_b, b_in2, 3, *rnn_args(direction), zeros8,
                    p=p_ctx, c=chunk, reverse=bool(direction), mod_row=1, write_states=False)
          for direction in (0, 1)]

    x3 = _permute_tokens(x[0], GRID_W)
    xr, hb = _rnn_pass(x3, mod, norm_g2, w_in_b, b_in2, 3, *rnn_args(1), h0[1],
                          p=GRID_W, c=chunk, reverse=True, mod_row=0, write_states=True)
    lc, lr, sgc, sgr = _mixer(x3, mod, norm_g2, w_in_b, b_in2, dw, conv_dw_b[l][None],
                              conv_ln_g[l][None], conv_ln_b[l][None], xr, hb, *rnn_args(0), h0[0],
                              p=GRID_W, c=chunk)
    out = _out_proj(lc, lr, sgc, sgr, x3.reshape(seq, d), mod, final_g[None],
                    w_conv_out[l].astype(BF16), w_rnn_out[l].astype(BF16), w_o[l].astype(BF16),
                    t=min(seq, 256))
    return _unpermute_tokens(out, GRID_W)[None]
```

```python
import functools
import math

import jax
import jax.numpy as jnp
from jax import lax
from jax.experimental import pallas as pl
from jax.experimental.pallas import tpu as pltpu

EPS = 1e-6
LRU_C = 8.0
GRID_W = 64
CONV_W = 31
CONV_HALF = CONV_W // 2
CONV_PAD = 16
CONV_BLOCK = 8
SHORT = 4
RNN_BW = 128

SUBLANES = 8
LANES = 128
VMEM_LIMIT_BYTES = 56 * 1024 * 1024

F32 = jnp.float32
BF16 = jnp.bfloat16


def _sigmoid(x):
    return jax.nn.sigmoid(x)


def _silu(x):
    return x * jax.nn.sigmoid(x)


def _always(i, k):
    return i + k >= 0


def _zero_from(x):
    u = pltpu.bitcast(x, jnp.uint32)
    z = lax.shift_right_logical(lax.shift_right_logical(u, jnp.uint32(16)), jnp.uint32(16))
    return pltpu.bitcast(z, F32)


def _rms_mod(x, g, scale, shift):
    ms = jnp.mean(x * x, axis=-1, keepdims=True)
    y = x * lax.rsqrt(ms + EPS)
    return y * (g * (1.0 + scale)) + shift


def _adaln_kernel(cb_ref, w_ref, b_ref, o_ref):
    d, bn = w_ref.shape
    rows = cb_ref.shape[0]
    o_ref[...] = jnp.zeros(o_ref.shape, F32)
    for r in range(rows):
        s3 = _silu(cb_ref[r]).reshape(d // SUBLANES, SUBLANES, LANES)
        for nb in range(bn // LANES):
            sl = slice(nb * LANES, (nb + 1) * LANES)
            w3 = w_ref[:, sl].reshape(d // SUBLANES, SUBLANES, LANES)
            part = jnp.sum(w3 * s3, axis=0)
            o_ref[r:r + 1, sl] = jnp.sum(part, axis=0, keepdims=True) + b_ref[:, sl]


def _adaln(cb, w, b):
    rows, d, _ = cb.shape
    n = w.shape[1]
    bn = min(d, 512)
    assert n % bn == 0 and rows <= SUBLANES
    return pl.pallas_call(
        _adaln_kernel,
        grid=(n // bn,),
        in_specs=[pl.BlockSpec((rows, d, LANES), lambda j: (0, 0, 0)),
                  pl.BlockSpec((d, bn), lambda j: (0, j)),
                  pl.BlockSpec((1, bn), lambda j: (0, j))],
        out_specs=pl.BlockSpec((SUBLANES, bn), lambda j: (0, j)),
        out_shape=jax.ShapeDtypeStruct((SUBLANES, n), F32),
        compiler_params=pltpu.CompilerParams(dimension_semantics=("arbitrary",),
                                             vmem_limit_bytes=VMEM_LIMIT_BYTES),
        name="adaln",
    )(cb, w, b)


def _halo(xv, nbr, reverse, sub):
    p = xv.shape[0]
    out = []
    for j in range(SHORT - 1):
        if reverse:
            own = pltpu.roll(xv[j], SUBLANES - 1, axis=0)
            oth = pltpu.roll(nbr[j], SUBLANES - 1, axis=0)
            out.append(jnp.where(sub == SUBLANES - 1, oth, own))
        else:
            own = pltpu.roll(xv[p - (SHORT - 1) + j], 1, axis=0)
            oth = pltpu.roll(nbr[j], 1, axis=0)
            out.append(jnp.where(sub == 0, oth, own))
    return jnp.stack(out, axis=0)


def _short_conv(xv, halo, cw_ref, cb, reverse):
    p = xv.shape[0]
    xe = jnp.concatenate([xv, halo] if reverse else [halo, xv], axis=0)
    acc = cb[None] + cw_ref[0][None] * xe[0:p]
    for k in range(1, SHORT):
        acc = acc + cw_ref[k][None] * xe[k:k + p]
    return acc


def _gates_ab(xc, wri_ref, br, bi, lam, a_ref, b_ref):
    p, _, c = xc.shape
    x2 = xc.reshape(p * SUBLANES, c)
    xb = x2.astype(BF16)
    neg = -lam
    softplus = jnp.maximum(neg, 0.0) + jnp.log(1.0 + jnp.exp(-jnp.abs(neg)))
    cl2 = softplus * (-LRU_C * math.log2(math.e))
    for h in range(c // RNN_BW):
        sl = slice(h * RNN_BW, (h + 1) * RNN_BW)
        pre = jnp.dot(xb[:, sl], wri_ref[h], preferred_element_type=F32)
        r = _sigmoid(pre[:, :RNN_BW] + br[:, sl])
        i = _sigmoid(pre[:, RNN_BW:] + bi[:, sl])
        a = jnp.exp2(r * cl2[:, sl])
        om = 1.0 - a * a
        mult = jnp.where(om > 0.0, om * lax.rsqrt(om), 0.0)
        b = mult * (i * x2[:, sl])
        a_ref[:, :, sl] = a.reshape(p, SUBLANES, RNN_BW)
        b_ref[:, :, sl] = b.reshape(p, SUBLANES, RNN_BW)


def _lru_scan(a_ref, b_ref, carry_prev, reverse, h_ref):
    p, _, c = a_ref.shape
    sub = lax.broadcasted_iota(jnp.int32, (SUBLANES, c), 0)
    order = range(p - 1, -1, -1) if reverse else range(p)
    b_end = jnp.zeros((SUBLANES, c), F32)
    a_end = jnp.ones((SUBLANES, c), F32)
    for t in order:
        a = a_ref[t]
        b_end = a * b_end + b_ref[t]
        a_end = a_end * a
    for s in (1, 2, 4):
        shift = (SUBLANES - s) if reverse else s
        a_sh = pltpu.roll(a_end, shift, axis=0)
        b_sh = pltpu.roll(b_end, shift, axis=0)
        valid = (sub < SUBLANES - s) if reverse else (sub >= s)
        b_end = jnp.where(valid, a_end * b_sh + b_end, b_end)
        a_end = jnp.where(valid, a_end * a_sh, a_end)
    last = 0 if reverse else SUBLANES - 1
    first = SUBLANES - 1 - last
    c_in = jnp.broadcast_to(carry_prev[last:last + 1, :], (SUBLANES, c))
    h_out = b_end + a_end * c_in
    if h_ref is not None:
        h = jnp.where(sub == first, c_in,
                      pltpu.roll(h_out, (SUBLANES - 1) if reverse else 1, axis=0))
        for t in order:
            h = a_ref[t] * h + b_ref[t]
            h_ref[t] = h
    return h_out


def _rnn_gates(xv, nbr_ref, cw_ref, cb, wri_ref, br, bi, lam, a_s, b_s, reverse):
    p, _, c = xv.shape
    sub = lax.broadcasted_iota(jnp.int32, (SUBLANES, c), 0)
    halo = _halo(xv, nbr_ref[...], reverse, sub)
    nbr_ref[...] = xv[0:SHORT - 1] if reverse else xv[p - (SHORT - 1):p]
    xc = _short_conv(xv, halo, cw_ref, cb, reverse)
    _gates_ab(xc, wri_ref, br, bi, lam, a_s, b_s)


def _rnn_scan(carry_ref, a_s, b_s, reverse, h_ref):
    h_out = _lru_scan(a_s, b_s, carry_ref[...], reverse, h_ref)
    carry_ref[...] = h_out
    return h_out


def _rnn_tiles_kernel(x_ref, shift_ref, scale_ref, g_ref, w_ref, bx_ref, cw_ref, cb_ref, wri_ref,
                      br_ref, bi_ref, lam_ref, h0_ref, *rest, reverse, mod_row, write_states, c):
    if write_states:
        xr_ref, h_ref, hn_s, a_s, b_s, nbr_s, carry_s = rest
        hfin_ref = None
    else:
        hfin_ref, hn_s, xr_ref, a_s, b_s, nbr_s, carry_s = rest
        h_ref = None
    i = pl.program_id(0)
    p, _, d = x_ref.shape
    dr = w_ref.shape[1]
    nc = dr // c
    nb = c // RNN_BW

    @pl.when(i == 0)
    def _():
        nbr_s[...] = jnp.zeros(nbr_s.shape, F32)
        carry_s[...] = h0_ref[...]

    def lanes(k):
        return slice(k * c, (k + 1) * c)

    def project(k):
        xr = jnp.dot(hn_s[...], w_ref[:, lanes(k)], preferred_element_type=F32) + bx_ref[:, lanes(k)]
        xr_ref[:, :, lanes(k)] = xr.reshape(p, SUBLANES, c)

    def gates(k):
        sl = lanes(k)
        _rnn_gates(xr_ref[:, :, sl], nbr_s.at[:, :, sl], cw_ref.at[:, :, sl], cb_ref[:, sl],
                   wri_ref.at[k * nb:(k + 1) * nb], br_ref[:, sl], bi_ref[:, sl], lam_ref[:, sl],
                   a_s, b_s, reverse)

    def scan(k):
        sl = lanes(k)
        h_end = _rnn_scan(carry_s.at[:, sl], a_s, b_s, reverse,
                          None if h_ref is None else h_ref.at[:, :, sl])
        if hfin_ref is not None:
            hfin_ref[:, sl] = h_end

    x = x_ref[...].reshape(p * SUBLANES, d)
    row = slice(mod_row, mod_row + 1)
    hn_s[...] = _rms_mod(x, g_ref[...], scale_ref[row, :], shift_ref[row, :]).astype(BF16)
    project(0)
    for k in range(nc):

        @pl.when(_always(i, k))
        def _(k=k):
            gates(k)
            if k + 1 < nc:
                project(k + 1)
            scan(k)


def _rnn_tiles(x3, mod, norm_g, w_in, b_in, xr_col, cw, cb, wri, br, bi, lam, h0, *, p, c, reverse,
               mod_row, write_states):
    s8, _, d = x3.shape
    dr = cw.shape[-1]
    nt = s8 // p
    t = p * SUBLANES

    def tile(i):
        return (nt - 1 - i) if reverse else i

    def whole(shape):
        return pl.BlockSpec(shape, lambda i: (0,) * len(shape))

    in_specs = [
        pl.BlockSpec((p, SUBLANES, d), lambda i: (tile(i), 0, 0)),
        pl.BlockSpec((SUBLANES, d), lambda i: (0, 0)),
        pl.BlockSpec((SUBLANES, d), lambda i: (0, 1)),
        whole((1, d)),
        pl.BlockSpec((d, dr), lambda i: (0, xr_col), pipeline_mode=pl.Buffered(1)),
        pl.BlockSpec((1, dr), lambda i: (0, xr_col)),
        whole((SHORT, SUBLANES, dr)), whole((SUBLANES, dr)),
        whole((dr // RNN_BW, RNN_BW, 2 * RNN_BW)),
        whole((1, dr)), whole((1, dr)), whole((1, dr)),
        whole((SUBLANES, dr)),
    ]
    state = jax.ShapeDtypeStruct((s8, SUBLANES, dr), F32)
    if write_states:
        st_spec = pl.BlockSpec((p, SUBLANES, dr), lambda i: (tile(i), 0, 0))
        out_specs = [st_spec, st_spec]
        out_shape = [state, state]
        xr_scratch = []
    else:
        assert nt == 1
        out_specs = whole((SUBLANES, dr))
        out_shape = jax.ShapeDtypeStruct((SUBLANES, dr), F32)
        xr_scratch = [pltpu.VMEM((p, SUBLANES, dr), F32)]
    scratch = ([pltpu.VMEM((t, d), BF16)] + xr_scratch + [
        pltpu.VMEM((p, SUBLANES, c), F32),
        pltpu.VMEM((p, SUBLANES, c), F32),
        pltpu.VMEM((SHORT - 1, SUBLANES, dr), F32),
        pltpu.VMEM((SUBLANES, dr), F32),
    ])
    kern = functools.partial(_rnn_tiles_kernel, reverse=reverse, mod_row=mod_row,
                             write_states=write_states, c=c)
    return pl.pallas_call(
        kern,
        grid=(nt,),
        in_specs=in_specs,
        out_specs=out_specs,
        out_shape=out_shape,
        scratch_shapes=scratch,
        compiler_params=pltpu.CompilerParams(dimension_semantics=("arbitrary",),
                                             vmem_limit_bytes=VMEM_LIMIT_BYTES),
        name="rnn_rev" if reverse else "rnn_fwd",
    )(x3, mod, mod, norm_g, w_in, b_in, cw, cb, wri, br, bi, lam, h0)


def _mixer_kernel(x_ref, shift_ref, scale_ref, g_ref,
                  wa_ref, wg_ref, wzc_ref, wzr_ref, wgc_ref, wgr_ref,
                  ba_ref, bg_ref, bzc_ref, bzr_ref, bgc_ref, bgr_ref,
                  dw_ref, dwb_ref,
                  xr_ref, hb_ref, cw_ref, cb_ref, wri_ref, br_ref, bi_ref, lam_ref, h0_ref,
                  cv_ref, szc_ref, lr_ref, sgc_ref, sgr_ref,
                  hn_s, vpad_s, a_s, b_s, hf_s, nbr_s, carry_s):
    i = pl.program_id(0)
    j = pl.program_id(1)
    p, _, d = x_ref.shape
    c = wa_ref.shape[1]
    t = p * SUBLANES

    @pl.when(j == 0)
    def _():
        x = x_ref[...].reshape(t, d)
        hn_s[...] = _rms_mod(x, g_ref[...], scale_ref[0:1, :], shift_ref[0:1, :]).astype(BF16)

    @pl.when((i == 0) & (j == 0))
    def _():
        zeros = jnp.zeros((CONV_PAD, SUBLANES, c), F32)
        vpad_s[0:CONV_PAD] = zeros
        vpad_s[CONV_PAD + p:CONV_PAD + p + CONV_PAD] = zeros

    @pl.when(i == 0)
    def _():
        nbr_s[j] = jnp.zeros(nbr_s.shape[1:], F32)
        carry_s[j] = h0_ref[...]

    def proj(w_ref, b_ref):
        return jnp.dot(hn_s[...], w_ref[...], preferred_element_type=F32) + b_ref[...]

    _rnn_gates(xr_ref[...], nbr_s.at[j], cw_ref, cb_ref[...], wri_ref, br_ref[...], bi_ref[...],
               lam_ref[...], a_s, b_s, False)
    v = proj(wa_ref, ba_ref) * _sigmoid(proj(wg_ref, bg_ref))
    vpad_s[CONV_PAD:CONV_PAD + p] = v.reshape(p, SUBLANES, c)

    @pl.when(_always(i, 0))
    def _():
        off = CONV_PAD - CONV_HALF
        pb = CONV_BLOCK
        for q in range(p // pb):
            for lb in range(c // LANES):
                sl = slice(lb * LANES, (lb + 1) * LANES)
                acc = jnp.broadcast_to(dwb_ref[:, sl][None], (pb, SUBLANES, LANES))
                for k in range(CONV_W):
                    lo_k = q * pb + k + off
                    acc = acc + dw_ref[k, :, sl][None] * vpad_s[lo_k:lo_k + pb, :, sl]
                cv_ref[q * pb:(q + 1) * pb, :, sl] = acc
        szc_ref[...] = _silu(proj(wzc_ref, bzc_ref)).astype(BF16)
        _rnn_scan(carry_s.at[j], a_s, b_s, False, hf_s)
        szr = _silu(proj(wzr_ref, bzr_ref))
        h = (hf_s[...] + hb_ref[...]).reshape(t, c)
        lr_ref[...] = (h * szr).astype(BF16)
        sgc_ref[...] = _sigmoid(proj(wgc_ref, bgc_ref))
        sgr_ref[...] = _sigmoid(proj(wgr_ref, bgr_ref))


def _mixer(x3, mod, norm_g, w_in, b_in, dw, dwb, xr, hb, cw, cb, wri, br, bi, lam, h0, *, p, c):
    s8, _, d = x3.shape
    nt = s8 // p
    nc = d // c
    t = p * SUBLANES
    s = s8 * SUBLANES

    def wspec(piece):
        return pl.BlockSpec((d, c), lambda i, j: (0, piece * nc + j))

    def bspec(piece):
        return pl.BlockSpec((1, c), lambda i, j: (0, piece * nc + j))

    def row(width):
        return pl.BlockSpec((1, width), lambda i, j: (0, 0))

    pieces = (0, 1, 2, 4, 5, 6)
    chunk_row = pl.BlockSpec((1, c), lambda i, j: (0, j))
    chunk_8 = pl.BlockSpec((SUBLANES, c), lambda i, j: (0, j))
    tok3 = pl.BlockSpec((p, SUBLANES, c), lambda i, j: (i, 0, j))
    tok2 = pl.BlockSpec((t, c), lambda i, j: (i, j))
    in_specs = (
        [pl.BlockSpec((p, SUBLANES, d), lambda i, j: (i, 0, 0)),
         pl.BlockSpec((SUBLANES, d), lambda i, j: (0, 0)),
         pl.BlockSpec((SUBLANES, d), lambda i, j: (0, 1)),
         row(d)]
        + [wspec(k) for k in pieces]
        + [bspec(k) for k in pieces]
        + [pl.BlockSpec((CONV_W, SUBLANES, c), lambda i, j: (0, 0, j)), chunk_row,
           tok3, tok3,
           pl.BlockSpec((SHORT, SUBLANES, c), lambda i, j: (0, 0, j)), chunk_8,
           pl.BlockSpec((c // RNN_BW, RNN_BW, 2 * RNN_BW), lambda i, j: (j, 0, 0)),
           chunk_row, chunk_row, chunk_row, chunk_8])
    out_specs = [tok3, tok2, tok2, tok2, tok2]
    out_shape = [jax.ShapeDtypeStruct((s8, SUBLANES, d), F32), jax.ShapeDtypeStruct((s, d), BF16),
                 jax.ShapeDtypeStruct((s, d), BF16),
                 jax.ShapeDtypeStruct((s, d), F32), jax.ShapeDtypeStruct((s, d), F32)]
    scratch = [
        pltpu.VMEM((t, d), BF16),
        pltpu.VMEM((p + 2 * CONV_PAD, SUBLANES, c), F32),
        pltpu.VMEM((p, SUBLANES, c), F32),
        pltpu.VMEM((p, SUBLANES, c), F32),
        pltpu.VMEM((p, SUBLANES, c), F32),
        pltpu.VMEM((nc, SHORT - 1, SUBLANES, c), F32),
        pltpu.VMEM((nc, SUBLANES, c), F32),
    ]
    args = ([x3, mod, mod, norm_g] + [w_in] * len(pieces) + [b_in] * len(pieces)
            + [dw, dwb, xr, hb, cw, cb, wri, br, bi, lam, h0])
    return pl.pallas_call(
        _mixer_kernel,
        grid=(nt, nc),
        in_specs=in_specs,
        out_specs=out_specs,
        out_shape=out_shape,
        scratch_shapes=scratch,
        compiler_params=pltpu.CompilerParams(dimension_semantics=("arbitrary", "arbitrary"),
                                             vmem_limit_bytes=VMEM_LIMIT_BYTES),
        name="mixer",
    )(*args)


def _out_kernel(cv_ref, szc_ref, lr_ref, sgc_ref, sgr_ref, x_ref, gate_ref, lng_ref, lnb_ref, fg_ref,
                wc_ref, wr_ref, wo_ref, o_ref):
    yr = jnp.dot(lr_ref[...], wr_ref[...], preferred_element_type=F32)
    cv = cv_ref[...]
    mu = jnp.mean(cv, axis=-1, keepdims=True)
    dev = cv - mu
    var = jnp.mean(dev * dev, axis=-1, keepdims=True)
    y = dev * lax.rsqrt(var + EPS) * lng_ref[...] + lnb_ref[...]
    lc = (_silu(y) * szc_ref[...].astype(F32)).astype(BF16)
    yc = jnp.dot(lc, wc_ref[...], preferred_element_type=F32)
    y = (sgc_ref[...] * yc + sgr_ref[...] * yr).astype(BF16)
    o = jnp.dot(y, wo_ref[...], preferred_element_type=F32)
    xn = x_ref[...] + gate_ref[0:1, :] * o
    ms = jnp.mean(xn * xn, axis=-1, keepdims=True)
    o_ref[...] = xn * lax.rsqrt(ms + EPS) * fg_ref[...]


def _out_proj(cv, szc, lr, sgc, sgr, x2, mod, ln_g, ln_b, final_g, wc, wr, wo, *, t):
    s, d = x2.shape
    tok = pl.BlockSpec((t, d), lambda i: (i, 0))
    row = pl.BlockSpec((1, d), lambda i: (0, 0))
    resident = pl.BlockSpec((d, d), lambda i: (0, 0), pipeline_mode=pl.Buffered(1))
    return pl.pallas_call(
        _out_kernel,
        grid=(s // t,),
        in_specs=[tok, tok, tok, tok, tok, tok,
                  pl.BlockSpec((SUBLANES, d), lambda i: (0, 2)),
                  row, row, row,
                  resident, resident, resident],
        out_specs=tok,
        out_shape=jax.ShapeDtypeStruct((s, d), F32),
        compiler_params=pltpu.CompilerParams(dimension_semantics=("arbitrary",),
                                             vmem_limit_bytes=VMEM_LIMIT_BYTES),
        name="out_proj",
    )(cv, szc, lr, sgc, sgr, x2, mod, ln_g, ln_b, final_g, wc, wr, wo)


def _permute_tokens(x2, p):
    s, d = x2.shape
    nt = s // (p * SUBLANES)
    return x2.reshape(nt, SUBLANES, p, d).transpose(0, 2, 1, 3).reshape(s // SUBLANES, SUBLANES, d)


def _unpermute_tokens(y2, p):
    s, d = y2.shape
    nt = s // (p * SUBLANES)
    return y2.reshape(nt, p, SUBLANES, d).transpose(0, 2, 1, 3).reshape(s, d)


def kernel(x, c, ctx, c_ctx, w_ada, b_ada, norm_g, w_in, b_in, conv_dw, conv_dw_b, conv_ln_g, conv_ln_b, w_conv_out, rnn_conv, rnn_conv_b, rnn_w_r, rnn_b_r, rnn_w_i, rnn_b_i, rnn_lam, w_rnn_out, w_o, final_g):
    bsz, seq, d = x.shape
    ctx_len = ctx.shape[1]
    depth = w_in.shape[0]
    assert bsz == 1 and depth == 1
    assert w_in.shape[2] == 7 * d and rnn_w_r.shape[3] == RNN_BW and conv_dw.shape[1] == CONV_W
    assert seq % (SUBLANES * GRID_W) == 0 and ctx_len % SUBLANES == 0 and d % LANES == 0
    l = 0
    chunk = min(d, 256)
    p_ctx = ctx_len // SUBLANES

    w_in_b = w_in[l].astype(BF16)
    b_in2 = b_in[l][None]
    wri = jnp.concatenate([rnn_w_r[l], rnn_w_i[l]], axis=-1).astype(BF16)
    cw = jnp.broadcast_to(rnn_conv[l][:, :, None, :], (2, SHORT, SUBLANES, d))
    cb = jnp.broadcast_to(rnn_conv_b[l][:, None, :], (2, SUBLANES, d))
    br = rnn_b_r[l][:, None, :]
    bi = rnn_b_i[l][:, None, :]
    lam = rnn_lam[l][:, None, :]
    dw = jnp.broadcast_to(conv_dw[l][:, None, :], (CONV_W, SUBLANES, d))
    norm_g2 = norm_g[l][None]

    cond = jnp.stack([c[0], c_ctx])
    mod = _adaln(jnp.broadcast_to(cond[:, :, None], (2, d, LANES)), w_ada[l], b_ada[l][None])

    def rnn_args(direction):
        return (cw[direction], cb[direction], wri[direction], br[direction], bi[direction],
                lam[direction])

    ctx3 = _permute_tokens(ctx[0], p_ctx)
    zeros8 = jnp.zeros((SUBLANES, d), F32)
    h0 = [_rnn_tiles(ctx3, mod, norm_g2, w_in_b, b_in2, 3, *rnn_args(direction), zeros8,
                     p=p_ctx, c=chunk, reverse=bool(direction), mod_row=1, write_states=False)
          for direction in (0, 1)]

    x3 = _permute_tokens(x[0], GRID_W)
    xr, hb = _rnn_tiles(x3, mod, norm_g2, w_in_b, b_in2, 3, *rnn_args(1), h0[1],
                        p=GRID_W, c=min(d, 512), reverse=True, mod_row=0, write_states=True)
    cv, szc, lr, sgc, sgr = _mixer(x3, mod, norm_g2, w_in_b, b_in2, dw, conv_dw_b[l][None], xr, hb,
                                   *rnn_args(0), h0[0], p=GRID_W, c=chunk)
    out = _out_proj(cv.reshape(seq, d), szc, lr, sgc, sgr, x3.reshape(seq, d), mod,
                    conv_ln_g[l][None], conv_ln_b[l][None], final_g[None],
                    w_conv_out[l].astype(BF16), w_rnn_out[l].astype(BF16), w_o[l].astype(BF16),
                    t=min(seq, 256))
    return _unpermute_tokens(out, GRID_W)[None]
```

```python
import functools
import math

import jax
import jax.numpy as jnp
from jax import lax
from jax.experimental import pallas as pl
from jax.experimental.pallas import tpu as pltpu

EPS = 1e-6
LRU_C = 8.0
GRID_W = 64
CONV_W = 31
CONV_HALF = CONV_W // 2
CONV_PAD = 16
CONV_BLOCK = 8
SHORT = 4
RNN_BW = 128

SUBLANES = 8
LANES = 128
VMEM_LIMIT_BYTES = 56 * 1024 * 1024

F32 = jnp.float32
BF16 = jnp.bfloat16


def _sigmoid(x):
    return jax.nn.sigmoid(x)


def _silu(x):
    return x * jax.nn.sigmoid(x)


def _always(i, k):
    return i + k >= 0


def _zero_from(x):
    u = pltpu.bitcast(x, jnp.uint32)
    z = lax.shift_right_logical(lax.shift_right_logical(u, jnp.uint32(16)), jnp.uint32(16))
    return pltpu.bitcast(z, F32)


def _rms_mod(x, g, scale, shift):
    ms = jnp.mean(x * x, axis=-1, keepdims=True)
    y = x * lax.rsqrt(ms + EPS)
    return y * (g * (1.0 + scale)) + shift


def _adaln_kernel(cb_ref, w_ref, b_ref, o_ref, acc_s):
    j = pl.program_id(0)
    bk, n = w_ref.shape
    rows = cb_ref.shape[0]

    @pl.when(j == 0)
    def _():
        acc_s[...] = jnp.zeros(acc_s.shape, F32)

    for r in range(rows):
        s3 = _silu(cb_ref[r]).reshape(bk // SUBLANES, SUBLANES, LANES)
        for nb in range(n // LANES):
            sl = slice(nb * LANES, (nb + 1) * LANES)
            w3 = w_ref[:, sl].reshape(bk // SUBLANES, SUBLANES, LANES)
            acc_s[r, :, sl] += jnp.sum(w3 * s3, axis=0)

    @pl.when(j == pl.num_programs(0) - 1)
    def _():
        o_ref[...] = jnp.zeros(o_ref.shape, F32)
        for r in range(rows):
            o_ref[r:r + 1, :] = jnp.sum(acc_s[r], axis=0, keepdims=True) + b_ref[...]


def _adaln(cb, w, b):
    rows, d, _ = cb.shape
    n = w.shape[1]
    bk = min(d, 128)
    assert d % bk == 0 and rows <= SUBLANES
    return pl.pallas_call(
        _adaln_kernel,
        grid=(d // bk,),
        in_specs=[pl.BlockSpec((rows, bk, LANES), lambda j: (0, j, 0)),
                  pl.BlockSpec((bk, n), lambda j: (j, 0)),
                  pl.BlockSpec((1, n), lambda j: (0, 0))],
        out_specs=pl.BlockSpec((SUBLANES, n), lambda j: (0, 0)),
        out_shape=jax.ShapeDtypeStruct((SUBLANES, n), F32),
        scratch_shapes=[pltpu.VMEM((rows, SUBLANES, n), F32)],
        compiler_params=pltpu.CompilerParams(dimension_semantics=("arbitrary",),
                                             vmem_limit_bytes=VMEM_LIMIT_BYTES),
        name="adaln",
    )(cb, w, b)


def _halo(xv, nbr, reverse, sub):
    p = xv.shape[0]
    out = []
    for j in range(SHORT - 1):
        if reverse:
            own = pltpu.roll(xv[j], SUBLANES - 1, axis=0)
            oth = pltpu.roll(nbr[j], SUBLANES - 1, axis=0)
            out.append(jnp.where(sub == SUBLANES - 1, oth, own))
        else:
            own = pltpu.roll(xv[p - (SHORT - 1) + j], 1, axis=0)
            oth = pltpu.roll(nbr[j], 1, axis=0)
            out.append(jnp.where(sub == 0, oth, own))
    return jnp.stack(out, axis=0)


def _short_conv(xv, halo, cw_ref, cb, reverse):
    p = xv.shape[0]
    xe = jnp.concatenate([xv, halo] if reverse else [halo, xv], axis=0)
    acc = cb[None] + cw_ref[0][None] * xe[0:p]
    for k in range(1, SHORT):
        acc = acc + cw_ref[k][None] * xe[k:k + p]
    return acc


def _gates_ab(xc, wri_ref, br, bi, lam, a_ref, b_ref):
    p, _, c = xc.shape
    x2 = xc.reshape(p * SUBLANES, c)
    xb = x2.astype(BF16)
    neg = -lam
    softplus = jnp.maximum(neg, 0.0) + jnp.log(1.0 + jnp.exp(-jnp.abs(neg)))
    cl2 = softplus * (-LRU_C * math.log2(math.e))
    for h in range(c // RNN_BW):
        sl = slice(h * RNN_BW, (h + 1) * RNN_BW)
        pre = jnp.dot(xb[:, sl], wri_ref[h], preferred_element_type=F32)
        r = _sigmoid(pre[:, :RNN_BW] + br[:, sl])
        i = _sigmoid(pre[:, RNN_BW:] + bi[:, sl])
        a = jnp.exp2(r * cl2[:, sl])
        om = 1.0 - a * a
        mult = jnp.where(om > 0.0, om * lax.rsqrt(om), 0.0)
        b = mult * (i * x2[:, sl])
        a_ref[:, :, sl] = a.reshape(p, SUBLANES, RNN_BW)
        b_ref[:, :, sl] = b.reshape(p, SUBLANES, RNN_BW)


def _lru_scan(a_ref, b_ref, carry_prev, reverse, h_ref):
    p, _, c = a_ref.shape
    sub = lax.broadcasted_iota(jnp.int32, (SUBLANES, c), 0)
    order = range(p - 1, -1, -1) if reverse else range(p)
    b_end = jnp.zeros((SUBLANES, c), F32)
    a_end = jnp.ones((SUBLANES, c), F32)
    for t in order:
        a = a_ref[t]
        b_end = a * b_end + b_ref[t]
        a_end = a_end * a
    for s in (1, 2, 4):
        shift = (SUBLANES - s) if reverse else s
        a_sh = pltpu.roll(a_end, shift, axis=0)
        b_sh = pltpu.roll(b_end, shift, axis=0)
        valid = (sub < SUBLANES - s) if reverse else (sub >= s)
        b_end = jnp.where(valid, a_end * b_sh + b_end, b_end)
        a_end = jnp.where(valid, a_end * a_sh, a_end)
    last = 0 if reverse else SUBLANES - 1
    first = SUBLANES - 1 - last
    c_in = jnp.broadcast_to(carry_prev[last:last + 1, :], (SUBLANES, c))
    h_out = b_end + a_end * c_in
    if h_ref is not None:
        h = jnp.where(sub == first, c_in,
                      pltpu.roll(h_out, (SUBLANES - 1) if reverse else 1, axis=0))
        for t in order:
            h = a_ref[t] * h + b_ref[t]
            h_ref[t] = h
    return h_out


def _rnn_gates(xv, nbr_ref, cw_ref, cb, wri_ref, br, bi, lam, a_s, b_s, reverse):
    p, _, c = xv.shape
    sub = lax.broadcasted_iota(jnp.int32, (SUBLANES, c), 0)
    halo = _halo(xv, nbr_ref[...], reverse, sub)
    nbr_ref[...] = xv[0:SHORT - 1] if reverse else xv[p - (SHORT - 1):p]
    xc = _short_conv(xv, halo, cw_ref, cb, reverse)
    _gates_ab(xc, wri_ref, br, bi, lam, a_s, b_s)


def _rnn_scan(carry_ref, a_s, b_s, reverse, h_ref):
    h_out = _lru_scan(a_s, b_s, carry_ref[...], reverse, h_ref)
    carry_ref[...] = h_out
    return h_out


def _rnn_tiles_kernel(x_ref, shift_ref, scale_ref, g_ref, w_ref, bx_ref, cw_ref, cb_ref, wri_ref,
                      br_ref, bi_ref, lam_ref, h0_ref, *rest, reverse, mod_row, write_states, c):
    if write_states:
        xr_ref, h_ref, hn_s, a_s, b_s, nbr_s, carry_s = rest
        hfin_ref = None
    else:
        hfin_ref, hn_s, xr_ref, a_s, b_s, nbr_s, carry_s = rest
        h_ref = None
    i = pl.program_id(0)
    p, _, d = x_ref.shape
    dr = w_ref.shape[1]
    nc = dr // c
    nb = c // RNN_BW

    @pl.when(i == 0)
    def _():
        nbr_s[...] = jnp.zeros(nbr_s.shape, F32)
        carry_s[...] = h0_ref[...]

    def lanes(k):
        return slice(k * c, (k + 1) * c)

    def project(k):
        xr = jnp.dot(hn_s[...], w_ref[:, lanes(k)], preferred_element_type=F32) + bx_ref[:, lanes(k)]
        xr_ref[:, :, lanes(k)] = xr.reshape(p, SUBLANES, c)

    def gates(k):
        sl = lanes(k)
        _rnn_gates(xr_ref[:, :, sl], nbr_s.at[:, :, sl], cw_ref.at[:, :, sl], cb_ref[:, sl],
                   wri_ref.at[k * nb:(k + 1) * nb], br_ref[:, sl], bi_ref[:, sl], lam_ref[:, sl],
                   a_s, b_s, reverse)

    def scan(k):
        sl = lanes(k)
        h_end = _rnn_scan(carry_s.at[:, sl], a_s, b_s, reverse,
                          None if h_ref is None else h_ref.at[:, :, sl])
        if hfin_ref is not None:
            hfin_ref[:, sl] = h_end

    x = x_ref[...].reshape(p * SUBLANES, d)
    row = slice(mod_row, mod_row + 1)
    hn_s[...] = _rms_mod(x, g_ref[...], scale_ref[row, :], shift_ref[row, :]).astype(BF16)
    project(0)
    for k in range(nc):

        @pl.when(_always(i, k))
        def _(k=k):
            gates(k)
            if k + 1 < nc:
                project(k + 1)
            scan(k)


def _rnn_tiles(x3, mod, norm_g, w_xr, b_xr, cw, cb, wri, br, bi, lam, h0, *, p, c, reverse,
               mod_row, write_states):
    s8, _, d = x3.shape
    dr = cw.shape[-1]
    nt = s8 // p
    t = p * SUBLANES

    def tile(i):
        return (nt - 1 - i) if reverse else i

    def whole(shape):
        return pl.BlockSpec(shape, lambda i: (0,) * len(shape))

    in_specs = [
        pl.BlockSpec((p, SUBLANES, d), lambda i: (tile(i), 0, 0)),
        pl.BlockSpec((SUBLANES, d), lambda i: (0, 0)),
        pl.BlockSpec((SUBLANES, d), lambda i: (0, 1)),
        whole((1, d)),
        pl.BlockSpec((d, dr), lambda i: (0, 0), pipeline_mode=pl.Buffered(1)),
        whole((1, dr)),
        whole((SHORT, SUBLANES, dr)), whole((SUBLANES, dr)),
        whole((dr // RNN_BW, RNN_BW, 2 * RNN_BW)),
        whole((1, dr)), whole((1, dr)), whole((1, dr)),
        whole((SUBLANES, dr)),
    ]
    state = jax.ShapeDtypeStruct((s8, SUBLANES, dr), F32)
    if write_states:
        st_spec = pl.BlockSpec((p, SUBLANES, dr), lambda i: (tile(i), 0, 0))
        out_specs = [st_spec, st_spec]
        out_shape = [state, state]
        xr_scratch = []
    else:
        assert nt == 1
        out_specs = whole((SUBLANES, dr))
        out_shape = jax.ShapeDtypeStruct((SUBLANES, dr), F32)
        xr_scratch = [pltpu.VMEM((p, SUBLANES, dr), F32)]
    scratch = ([pltpu.VMEM((t, d), BF16)] + xr_scratch + [
        pltpu.VMEM((p, SUBLANES, c), F32),
        pltpu.VMEM((p, SUBLANES, c), F32),
        pltpu.VMEM((SHORT - 1, SUBLANES, dr), F32),
        pltpu.VMEM((SUBLANES, dr), F32),
    ])
    kern = functools.partial(_rnn_tiles_kernel, reverse=reverse, mod_row=mod_row,
                             write_states=write_states, c=c)
    return pl.pallas_call(
        kern,
        grid=(nt,),
        in_specs=in_specs,
        out_specs=out_specs,
        out_shape=out_shape,
        scratch_shapes=scratch,
        compiler_params=pltpu.CompilerParams(dimension_semantics=("arbitrary",),
                                             vmem_limit_bytes=VMEM_LIMIT_BYTES),
        name="rnn_rev" if reverse else "rnn_fwd",
    )(x3, mod, mod, norm_g, w_xr, b_xr, cw, cb, wri, br, bi, lam, h0)


def _mixer_kernel(x_ref, shift_ref, scale_ref, g_ref, w_ref, bias_ref, dw_ref, dwb_ref,
                  xr_ref, hb_ref, cw_ref, cb_ref, wri_ref, br_ref, bi_ref, lam_ref, h0_ref,
                  cv_ref, szc_ref, lr_ref, sgc_ref, sgr_ref,
                  hn_s, vpad_s, a_s, b_s, hf_s, nbr_s, carry_s):
    i = pl.program_id(0)
    j = pl.program_id(1)
    p, _, d = x_ref.shape
    c = xr_ref.shape[2]
    t = p * SUBLANES

    @pl.when(j == 0)
    def _():
        x = x_ref[...].reshape(t, d)
        hn_s[...] = _rms_mod(x, g_ref[...], scale_ref[0:1, :], shift_ref[0:1, :]).astype(BF16)

    @pl.when((i == 0) & (j == 0))
    def _():
        zeros = jnp.zeros((CONV_PAD, SUBLANES, c), F32)
        vpad_s[0:CONV_PAD] = zeros
        vpad_s[CONV_PAD + p:CONV_PAD + p + CONV_PAD] = zeros

    @pl.when(i == 0)
    def _():
        nbr_s[j] = jnp.zeros(nbr_s.shape[1:], F32)
        carry_s[j] = h0_ref[...]

    def proj(piece):
        sl = slice(piece * c, (piece + 1) * c)
        return jnp.dot(hn_s[...], w_ref[:, sl], preferred_element_type=F32) + bias_ref[:, sl]

    _rnn_gates(xr_ref[...], nbr_s.at[j], cw_ref, cb_ref[...], wri_ref, br_ref[...], bi_ref[...],
               lam_ref[...], a_s, b_s, False)
    v = proj(0) * _sigmoid(proj(1))
    vpad_s[CONV_PAD:CONV_PAD + p] = v.reshape(p, SUBLANES, c)

    @pl.when(_always(i, 0))
    def _():
        off = CONV_PAD - CONV_HALF
        pb = CONV_BLOCK
        for q in range(p // pb):
            for lb in range(c // LANES):
                sl = slice(lb * LANES, (lb + 1) * LANES)
                acc = jnp.broadcast_to(dwb_ref[:, sl][None], (pb, SUBLANES, LANES))
                for k in range(CONV_W):
                    lo_k = q * pb + k + off
                    acc = acc + dw_ref[k, :, sl][None] * vpad_s[lo_k:lo_k + pb, :, sl]
                cv_ref[q * pb:(q + 1) * pb, :, sl] = acc
        szc_ref[...] = _silu(proj(2)).astype(BF16)
        _rnn_scan(carry_s.at[j], a_s, b_s, False, hf_s)
        szr = _silu(proj(3))
        h = (hf_s[...] + hb_ref[...]).reshape(t, c)
        lr_ref[...] = (h * szr).astype(BF16)
        sgc_ref[...] = _sigmoid(proj(4)).astype(BF16)
        sgr_ref[...] = _sigmoid(proj(5)).astype(BF16)


def _mixer(x3, mod, norm_g, w_chunks, b_chunks, dw, dwb, xr, hb, cw, cb, wri, br, bi, lam, h0, *, p):
    s8, _, d = x3.shape
    nc, _, c6 = w_chunks.shape
    c = c6 // 6
    nt = s8 // p
    t = p * SUBLANES
    s = s8 * SUBLANES
    assert nc * c == d

    def row(width):
        return pl.BlockSpec((1, width), lambda i, j: (0, 0))

    chunk_row = pl.BlockSpec((1, c), lambda i, j: (0, j))
    chunk_8 = pl.BlockSpec((SUBLANES, c), lambda i, j: (0, j))
    tok3 = pl.BlockSpec((p, SUBLANES, c), lambda i, j: (i, 0, j))
    tok2 = pl.BlockSpec((t, c), lambda i, j: (i, j))
    in_specs = (
        [pl.BlockSpec((p, SUBLANES, d), lambda i, j: (i, 0, 0)),
         pl.BlockSpec((SUBLANES, d), lambda i, j: (0, 0)),
         pl.BlockSpec((SUBLANES, d), lambda i, j: (0, 1)),
         row(d),
         pl.BlockSpec((None, d, c6), lambda i, j: (j, 0, 0)),
         pl.BlockSpec((None, 1, c6), lambda i, j: (j, 0, 0)),
         pl.BlockSpec((CONV_W, SUBLANES, c), lambda i, j: (0, 0, j)), chunk_row,
         tok3, tok3,
         pl.BlockSpec((SHORT, SUBLANES, c), lambda i, j: (0, 0, j)), chunk_8,
         pl.BlockSpec((c // RNN_BW, RNN_BW, 2 * RNN_BW), lambda i, j: (j, 0, 0)),
         chunk_row, chunk_row, chunk_row, chunk_8])
    out_specs = [tok3, tok2, tok2, tok2, tok2]
    out_shape = [jax.ShapeDtypeStruct((s8, SUBLANES, d), F32)] + [jax.ShapeDtypeStruct((s, d), BF16)] * 4
    scratch = [
        pltpu.VMEM((t, d), BF16),
        pltpu.VMEM((p + 2 * CONV_PAD, SUBLANES, c), F32),
        pltpu.VMEM((p, SUBLANES, c), F32),
        pltpu.VMEM((p, SUBLANES, c), F32),
        pltpu.VMEM((p, SUBLANES, c), F32),
        pltpu.VMEM((nc, SHORT - 1, SUBLANES, c), F32),
        pltpu.VMEM((nc, SUBLANES, c), F32),
    ]
    args = [x3, mod, mod, norm_g, w_chunks, b_chunks, dw, dwb, xr, hb, cw, cb, wri, br, bi, lam, h0]
    return pl.pallas_call(
        _mixer_kernel,
        grid=(nt, nc),
        in_specs=in_specs,
        out_specs=out_specs,
        out_shape=out_shape,
        scratch_shapes=scratch,
        compiler_params=pltpu.CompilerParams(dimension_semantics=("arbitrary", "arbitrary"),
                                             vmem_limit_bytes=VMEM_LIMIT_BYTES),
        name="mixer",
    )(*args)


def _out_kernel(cv_ref, szc_ref, lr_ref, sgc_ref, sgr_ref, x_ref, gate_ref, lng_ref, lnb_ref, fg_ref,
                wc_ref, wr_ref, wo_ref, o_ref):
    yr = jnp.dot(lr_ref[...], wr_ref[...], preferred_element_type=F32)
    cv = cv_ref[...]
    mu = jnp.mean(cv, axis=-1, keepdims=True)
    dev = cv - mu
    var = jnp.mean(dev * dev, axis=-1, keepdims=True)
    y = dev * lax.rsqrt(var + EPS) * lng_ref[...] + lnb_ref[...]
    lc = (_silu(y) * szc_ref[...].astype(F32)).astype(BF16)
    yc = jnp.dot(lc, wc_ref[...], preferred_element_type=F32)
    y = (sgc_ref[...].astype(F32) * yc + sgr_ref[...].astype(F32) * yr).astype(BF16)
    o = jnp.dot(y, wo_ref[...], preferred_element_type=F32)
    xn = x_ref[...] + gate_ref[0:1, :] * o
    ms = jnp.mean(xn * xn, axis=-1, keepdims=True)
    o_ref[...] = xn * lax.rsqrt(ms + EPS) * fg_ref[...]


def _out_proj(cv, szc, lr, sgc, sgr, x2, mod, ln_g, ln_b, final_g, wc, wr, wo, *, t):
    s, d = x2.shape
    tok = pl.BlockSpec((t, d), lambda i: (i, 0))
    row = pl.BlockSpec((1, d), lambda i: (0, 0))
    resident = pl.BlockSpec((d, d), lambda i: (0, 0), pipeline_mode=pl.Buffered(1))
    return pl.pallas_call(
        _out_kernel,
        grid=(s // t,),
        in_specs=[tok, tok, tok, tok, tok, tok,
                  pl.BlockSpec((SUBLANES, d), lambda i: (0, 2)),
                  row, row, row,
                  resident, resident, resident],
        out_specs=tok,
        out_shape=jax.ShapeDtypeStruct((s, d), F32),
        compiler_params=pltpu.CompilerParams(dimension_semantics=("arbitrary",),
                                             vmem_limit_bytes=VMEM_LIMIT_BYTES),
        name="out_proj",
    )(cv, szc, lr, sgc, sgr, x2, mod, ln_g, ln_b, final_g, wc, wr, wo)


def _permute_tokens(x2, p):
    s, d = x2.shape
    nt = s // (p * SUBLANES)
    return x2.reshape(nt, SUBLANES, p, d).transpose(0, 2, 1, 3).reshape(s // SUBLANES, SUBLANES, d)


def _unpermute_tokens(y2, p):
    s, d = y2.shape
    nt = s // (p * SUBLANES)
    return y2.reshape(nt, p, SUBLANES, d).transpose(0, 2, 1, 3).reshape(s, d)


def kernel(x, c, ctx, c_ctx, w_ada, b_ada, norm_g, w_in, b_in, conv_dw, conv_dw_b, conv_ln_g, conv_ln_b, w_conv_out, rnn_conv, rnn_conv_b, rnn_w_r, rnn_b_r, rnn_w_i, rnn_b_i, rnn_lam, w_rnn_out, w_o, final_g):
    bsz, seq, d = x.shape
    ctx_len = ctx.shape[1]
    depth = w_in.shape[0]
    assert bsz == 1 and depth == 1
    assert w_in.shape[2] == 7 * d and rnn_w_r.shape[3] == RNN_BW and conv_dw.shape[1] == CONV_W
    assert seq % (SUBLANES * GRID_W) == 0 and ctx_len % SUBLANES == 0 and d % LANES == 0
    l = 0
    chunk = min(d, 256)
    p_ctx = ctx_len // SUBLANES

    nc = d // chunk
    w7 = w_in[l].reshape(d, 7, nc, chunk)
    b7 = b_in[l].reshape(7, nc, chunk)
    main = jnp.array([0, 1, 2, 4, 5, 6])
    w_chunks = w7[:, main].transpose(2, 0, 1, 3).reshape(nc, d, 6 * chunk).astype(BF16)
    b_chunks = b7[main].transpose(1, 0, 2).reshape(nc, 1, 6 * chunk)
    w_xr = w7[:, 3].reshape(d, d).astype(BF16)
    b_xr = b7[3].reshape(1, d)
    wri = jnp.concatenate([rnn_w_r[l], rnn_w_i[l]], axis=-1).astype(BF16)
    cw = jnp.broadcast_to(rnn_conv[l][:, :, None, :], (2, SHORT, SUBLANES, d))
    cb = jnp.broadcast_to(rnn_conv_b[l][:, None, :], (2, SUBLANES, d))
    br = rnn_b_r[l][:, None, :]
    bi = rnn_b_i[l][:, None, :]
    lam = rnn_lam[l][:, None, :]
    dw = jnp.broadcast_to(conv_dw[l][:, None, :], (CONV_W, SUBLANES, d))
    norm_g2 = norm_g[l][None]

    cond = jnp.stack([c[0], c_ctx])
    mod = _adaln(jnp.broadcast_to(cond[:, :, None], (2, d, LANES)), w_ada[l], b_ada[l][None])

    def rnn_args(direction):
        return (cw[direction], cb[direction], wri[direction], br[direction], bi[direction],
                lam[direction])

    ctx3 = _permute_tokens(ctx[0], p_ctx)
    zeros8 = jnp.zeros((SUBLANES, d), F32)
    h0 = [_rnn_tiles(ctx3, mod, norm_g2, w_xr, b_xr, *rnn_args(direction), zeros8,
                     p=p_ctx, c=chunk, reverse=bool(direction), mod_row=1, write_states=False)
          for direction in (0, 1)]

    x3 = _permute_tokens(x[0], GRID_W)
    xr, hb = _rnn_tiles(x3, mod, norm_g2, w_xr, b_xr, *rnn_args(1), h0[1],
                        p=GRID_W, c=min(d, 512), reverse=True, mod_row=0, write_states=True)
    cv, szc, lr, sgc, sgr = _mixer(x3, mod, norm_g2, w_chunks, b_chunks, dw, conv_dw_b[l][None], xr, hb,
                                   *rnn_args(0), h0[0], p=GRID_W)
    out = _out_proj(cv.reshape(seq, d), szc, lr, sgc, sgr, x3.reshape(seq, d), mod,
                    conv_ln_g[l][None], conv_ln_b[l][None], final_g[None],
                    w_conv_out[l].astype(BF16), w_rnn_out[l].astype(BF16), w_o[l].astype(BF16),
                    t=min(seq, 256))
    return _unpermute_tokens(out, GRID_W)[None]
```

```python
import functools
import math

import jax
import jax.numpy as jnp
from jax import lax
from jax.experimental import pallas as pl
from jax.experimental.pallas import tpu as pltpu

EPS = 1e-6
LRU_C = 8.0
GRID_W = 64
CONV_W = 31
CONV_HALF = CONV_W // 2
CONV_PAD = 16
CONV_BLOCK = 8
SHORT = 4
RNN_BW = 128

SUBLANES = 8
LANES = 128
VMEM_LIMIT_BYTES = 56 * 1024 * 1024

F32 = jnp.float32
BF16 = jnp.bfloat16


def _sigmoid(x):
    return jax.nn.sigmoid(x)


def _silu(x):
    return x * jax.nn.sigmoid(x)


def _always(i, k):
    return i + k >= 0


def _zero_from(x):
    u = pltpu.bitcast(x, jnp.uint32)
    z = lax.shift_right_logical(lax.shift_right_logical(u, jnp.uint32(16)), jnp.uint32(16))
    return pltpu.bitcast(z, F32)


def _rms_mod(x, g, scale, shift):
    ms = jnp.mean(x * x, axis=-1, keepdims=True)
    y = x * lax.rsqrt(ms + EPS)
    return y * (g * (1.0 + scale)) + shift


def _adaln_kernel(cb_ref, w_ref, b_ref, o_ref, acc_s):
    j = pl.program_id(0)
    bk, n = w_ref.shape
    rows = cb_ref.shape[0]

    @pl.when(j == 0)
    def _():
        acc_s[...] = jnp.zeros(acc_s.shape, F32)

    for r in range(rows):
        s3 = _silu(cb_ref[r]).reshape(bk // SUBLANES, SUBLANES, LANES)
        for nb in range(n // LANES):
            sl = slice(nb * LANES, (nb + 1) * LANES)
            w3 = w_ref[:, sl].reshape(bk // SUBLANES, SUBLANES, LANES)
            acc_s[r, :, sl] += jnp.sum(w3 * s3, axis=0)

    @pl.when(j == pl.num_programs(0) - 1)
    def _():
        o_ref[...] = jnp.zeros(o_ref.shape, F32)
        for r in range(rows):
            o_ref[r:r + 1, :] = jnp.sum(acc_s[r], axis=0, keepdims=True) + b_ref[...]


def _adaln(cb, w, b):
    rows, d, _ = cb.shape
    n = w.shape[1]
    bk = min(d, 128)
    assert d % bk == 0 and rows <= SUBLANES
    return pl.pallas_call(
        _adaln_kernel,
        grid=(d // bk,),
        in_specs=[pl.BlockSpec((rows, bk, LANES), lambda j: (0, j, 0)),
                  pl.BlockSpec((bk, n), lambda j: (j, 0)),
                  pl.BlockSpec((1, n), lambda j: (0, 0))],
        out_specs=pl.BlockSpec((SUBLANES, n), lambda j: (0, 0)),
        out_shape=jax.ShapeDtypeStruct((SUBLANES, n), F32),
        scratch_shapes=[pltpu.VMEM((rows, SUBLANES, n), F32)],
        compiler_params=pltpu.CompilerParams(dimension_semantics=("arbitrary",),
                                             vmem_limit_bytes=VMEM_LIMIT_BYTES),
        name="adaln",
    )(cb, w, b)


def _halo(xv, nbr, reverse, sub):
    p = xv.shape[0]
    out = []
    for j in range(SHORT - 1):
        if reverse:
            own = pltpu.roll(xv[j], SUBLANES - 1, axis=0)
            oth = pltpu.roll(nbr[j], SUBLANES - 1, axis=0)
            out.append(jnp.where(sub == SUBLANES - 1, oth, own))
        else:
            own = pltpu.roll(xv[p - (SHORT - 1) + j], 1, axis=0)
            oth = pltpu.roll(nbr[j], 1, axis=0)
            out.append(jnp.where(sub == 0, oth, own))
    return jnp.stack(out, axis=0)


def _short_conv(xv, halo, cw_ref, cb, reverse):
    p = xv.shape[0]
    xe = jnp.concatenate([xv, halo] if reverse else [halo, xv], axis=0)
    acc = cb[None] + cw_ref[0][None] * xe[0:p]
    for k in range(1, SHORT):
        acc = acc + cw_ref[k][None] * xe[k:k + p]
    return acc


def _gates_ab(xc, wri_ref, br, bi, lam, a_ref, b_ref):
    p, _, c = xc.shape
    x2 = xc.reshape(p * SUBLANES, c)
    xb = x2.astype(BF16)
    neg = -lam
    softplus = jnp.maximum(neg, 0.0) + jnp.log(1.0 + jnp.exp(-jnp.abs(neg)))
    cl2 = softplus * (-LRU_C * math.log2(math.e))
    for h in range(c // RNN_BW):
        sl = slice(h * RNN_BW, (h + 1) * RNN_BW)
        pre = jnp.dot(xb[:, sl], wri_ref[h], preferred_element_type=F32)
        r = _sigmoid(pre[:, :RNN_BW] + br[:, sl])
        i = _sigmoid(pre[:, RNN_BW:] + bi[:, sl])
        a = jnp.exp2(r * cl2[:, sl])
        om = 1.0 - a * a
        mult = jnp.where(om > 0.0, om * lax.rsqrt(om), 0.0)
        b = mult * (i * x2[:, sl])
        a_ref[:, :, sl] = a.reshape(p, SUBLANES, RNN_BW)
        b_ref[:, :, sl] = b.reshape(p, SUBLANES, RNN_BW)


def _lru_scan(a_ref, b_ref, carry_prev, reverse, h_ref):
    p, _, c = a_ref.shape
    sub = lax.broadcasted_iota(jnp.int32, (SUBLANES, c), 0)
    order = range(p - 1, -1, -1) if reverse else range(p)
    b_end = jnp.zeros((SUBLANES, c), F32)
    a_end = jnp.ones((SUBLANES, c), F32)
    for t in order:
        a = a_ref[t]
        b_end = a * b_end + b_ref[t]
        a_end = a_end * a
    for s in (1, 2, 4):
        shift = (SUBLANES - s) if reverse else s
        a_sh = pltpu.roll(a_end, shift, axis=0)
        b_sh = pltpu.roll(b_end, shift, axis=0)
        valid = (sub < SUBLANES - s) if reverse else (sub >= s)
        b_end = jnp.where(valid, a_end * b_sh + b_end, b_end)
        a_end = jnp.where(valid, a_end * a_sh, a_end)
    last = 0 if reverse else SUBLANES - 1
    first = SUBLANES - 1 - last
    c_in = jnp.broadcast_to(carry_prev[last:last + 1, :], (SUBLANES, c))
    h_out = b_end + a_end * c_in
    if h_ref is not None:
        h = jnp.where(sub == first, c_in,
                      pltpu.roll(h_out, (SUBLANES - 1) if reverse else 1, axis=0))
        for t in order:
            h = a_ref[t] * h + b_ref[t]
            h_ref[t] = h
    return h_out


def _rnn_gates(xv, nbr_ref, cw_ref, cb, wri_ref, br, bi, lam, a_s, b_s, reverse):
    p, _, c = xv.shape
    sub = lax.broadcasted_iota(jnp.int32, (SUBLANES, c), 0)
    halo = _halo(xv, nbr_ref[...], reverse, sub)
    nbr_ref[...] = xv[0:SHORT - 1] if reverse else xv[p - (SHORT - 1):p]
    xc = _short_conv(xv, halo, cw_ref, cb, reverse)
    _gates_ab(xc, wri_ref, br, bi, lam, a_s, b_s)


def _rnn_scan(carry_ref, a_s, b_s, reverse, h_ref):
    h_out = _lru_scan(a_s, b_s, carry_ref[...], reverse, h_ref)
    carry_ref[...] = h_out
    return h_out


def _rnn_tiles_kernel(x_ref, shift_ref, scale_ref, g_ref, w_ref, bx_ref, cw_ref, cb_ref, wri_ref,
                      br_ref, bi_ref, lam_ref, h0_ref, *rest, reverse, mod_row, write_states, c):
    if write_states:
        xr_ref, h_ref, hn_s, a_s, b_s, nbr_s, carry_s = rest
        hfin_ref = None
    else:
        hfin_ref, hn_s, xr_ref, a_s, b_s, nbr_s, carry_s = rest
        h_ref = None
    i = pl.program_id(0)
    p, _, d = x_ref.shape
    dr = w_ref.shape[1]
    nc = dr // c
    nb = c // RNN_BW

    @pl.when(i == 0)
    def _():
        nbr_s[...] = jnp.zeros(nbr_s.shape, F32)
        carry_s[...] = h0_ref[...]

    def lanes(k):
        return slice(k * c, (k + 1) * c)

    def project(k):
        xr = jnp.dot(hn_s[...], w_ref[:, lanes(k)], preferred_element_type=F32) + bx_ref[:, lanes(k)]
        xr_ref[:, :, lanes(k)] = xr.reshape(p, SUBLANES, c)

    def gates(k):
        sl = lanes(k)
        _rnn_gates(xr_ref[:, :, sl], nbr_s.at[:, :, sl], cw_ref.at[:, :, sl], cb_ref[:, sl],
                   wri_ref.at[k * nb:(k + 1) * nb], br_ref[:, sl], bi_ref[:, sl], lam_ref[:, sl],
                   a_s, b_s, reverse)

    def scan(k):
        sl = lanes(k)
        h_end = _rnn_scan(carry_s.at[:, sl], a_s, b_s, reverse,
                          None if h_ref is None else h_ref.at[:, :, sl])
        if hfin_ref is not None:
            hfin_ref[:, sl] = h_end

    x = x_ref[...].reshape(p * SUBLANES, d)
    row = slice(mod_row, mod_row + 1)
    hn_s[...] = _rms_mod(x, g_ref[...], scale_ref[row, :], shift_ref[row, :]).astype(BF16)
    project(0)
    for k in range(nc):

        @pl.when(_always(i, k))
        def _(k=k):
            gates(k)
            if k + 1 < nc:
                project(k + 1)
            scan(k)


def _rnn_tiles(x3, mod, norm_g, w_xr, b_xr, cw, cb, wri, br, bi, lam, h0, *, p, c, reverse,
               mod_row, write_states):
    s8, _, d = x3.shape
    dr = cw.shape[-1]
    nt = s8 // p
    t = p * SUBLANES

    def tile(i):
        return (nt - 1 - i) if reverse else i

    def whole(shape):
        return pl.BlockSpec(shape, lambda i: (0,) * len(shape))

    in_specs = [
        pl.BlockSpec((p, SUBLANES, d), lambda i: (tile(i), 0, 0)),
        pl.BlockSpec((SUBLANES, d), lambda i: (0, 0)),
        pl.BlockSpec((SUBLANES, d), lambda i: (0, 1)),
        whole((1, d)),
        pl.BlockSpec((d, dr), lambda i: (0, 0), pipeline_mode=pl.Buffered(1)),
        whole((1, dr)),
        whole((SHORT, SUBLANES, dr)), whole((SUBLANES, dr)),
        whole((dr // RNN_BW, RNN_BW, 2 * RNN_BW)),
        whole((1, dr)), whole((1, dr)), whole((1, dr)),
        whole((SUBLANES, dr)),
    ]
    state = jax.ShapeDtypeStruct((s8, SUBLANES, dr), F32)
    if write_states:
        st_spec = pl.BlockSpec((p, SUBLANES, dr), lambda i: (tile(i), 0, 0))
        out_specs = [st_spec, st_spec]
        out_shape = [state, state]
        xr_scratch = []
    else:
        assert nt == 1
        out_specs = whole((SUBLANES, dr))
        out_shape = jax.ShapeDtypeStruct((SUBLANES, dr), F32)
        xr_scratch = [pltpu.VMEM((p, SUBLANES, dr), F32)]
    scratch = ([pltpu.VMEM((t, d), BF16)] + xr_scratch + [
        pltpu.VMEM((p, SUBLANES, c), F32),
        pltpu.VMEM((p, SUBLANES, c), F32),
        pltpu.VMEM((SHORT - 1, SUBLANES, dr), F32),
        pltpu.VMEM((SUBLANES, dr), F32),
    ])
    kern = functools.partial(_rnn_tiles_kernel, reverse=reverse, mod_row=mod_row,
                             write_states=write_states, c=c)
    return pl.pallas_call(
        kern,
        grid=(nt,),
        in_specs=in_specs,
        out_specs=out_specs,
        out_shape=out_shape,
        scratch_shapes=scratch,
        compiler_params=pltpu.CompilerParams(dimension_semantics=("arbitrary",),
                                             vmem_limit_bytes=VMEM_LIMIT_BYTES),
        name="rnn_rev" if reverse else "rnn_fwd",
    )(x3, mod, mod, norm_g, w_xr, b_xr, cw, cb, wri, br, bi, lam, h0)


def _mixer_kernel(x_ref, shift_ref, scale_ref, g_ref,
                  wa_ref, wg_ref, wzc_ref, wzr_ref, wgc_ref, wgr_ref,
                  ba_ref, bg_ref, bzc_ref, bzr_ref, bgc_ref, bgr_ref,
                  dw_ref, dwb_ref,
                  xr_ref, hb_ref, cw_ref, cb_ref, wri_ref, br_ref, bi_ref, lam_ref, h0_ref,
                  cv_ref, szc_ref, lr_ref, sgc_ref, sgr_ref,
                  hn_s, vpad_s, a_s, b_s, hf_s, nbr_s, carry_s, *, p):
    i = pl.program_id(0)
    j = pl.program_id(1)
    gp, _, d = x_ref.shape
    groups = gp // p
    c = wa_ref.shape[1]
    t = gp * SUBLANES

    @pl.when(j == 0)
    def _():
        x = x_ref[...].reshape(t, d)
        hn_s[...] = _rms_mod(x, g_ref[...], scale_ref[0:1, :], shift_ref[0:1, :]).astype(BF16)

    @pl.when((i == 0) & (j == 0))
    def _():
        zeros = jnp.zeros((CONV_PAD, SUBLANES, c), F32)
        for g in range(groups):
            vpad_s[g, 0:CONV_PAD] = zeros
            vpad_s[g, CONV_PAD + p:CONV_PAD + p + CONV_PAD] = zeros

    @pl.when(i == 0)
    def _():
        nbr_s[j] = jnp.zeros(nbr_s.shape[1:], F32)
        carry_s[j] = h0_ref[...]

    def proj(w_ref, b_ref):
        return jnp.dot(hn_s[...], w_ref[...], preferred_element_type=F32) + b_ref[...]

    for g in range(groups):
        _rnn_gates(xr_ref[g * p:(g + 1) * p], nbr_s.at[j], cw_ref, cb_ref[...], wri_ref, br_ref[...],
                   bi_ref[...], lam_ref[...], a_s.at[g], b_s.at[g], False)
    v = (proj(wa_ref, ba_ref) * _sigmoid(proj(wg_ref, bg_ref))).reshape(gp, SUBLANES, c)
    for g in range(groups):
        vpad_s[g, CONV_PAD:CONV_PAD + p] = v[g * p:(g + 1) * p]

    @pl.when(_always(i, 0))
    def _():
        off = CONV_PAD - CONV_HALF
        pb = CONV_BLOCK
        for g in range(groups):
            for q in range(p // pb):
                for lb in range(c // LANES):
                    sl = slice(lb * LANES, (lb + 1) * LANES)
                    acc = jnp.broadcast_to(dwb_ref[:, sl][None], (pb, SUBLANES, LANES))
                    for k in range(CONV_W):
                        lo_k = q * pb + k + off
                        acc = acc + dw_ref[k, :, sl][None] * vpad_s[g, lo_k:lo_k + pb, :, sl]
                    cv_ref[g * p + q * pb:g * p + (q + 1) * pb, :, sl] = acc
        szc_ref[...] = _silu(proj(wzc_ref, bzc_ref)).astype(BF16)
        for g in range(groups):
            _rnn_scan(carry_s.at[j], a_s.at[g], b_s.at[g], False, hf_s.at[g * p:(g + 1) * p])
        szr = _silu(proj(wzr_ref, bzr_ref))
        h = (hf_s[...] + hb_ref[...]).reshape(t, c)
        lr_ref[...] = (h * szr).astype(BF16)
        sgc_ref[...] = _sigmoid(proj(wgc_ref, bgc_ref)).astype(BF16)
        sgr_ref[...] = _sigmoid(proj(wgr_ref, bgr_ref)).astype(BF16)


def _mixer(x3, mod, norm_g, w_in, b_in, dw, dwb, xr, hb, cw, cb, wri, br, bi, lam, h0, *, p, groups, c):
    s8, _, d = x3.shape
    gp = groups * p
    nt = s8 // gp
    nc = d // c
    t = gp * SUBLANES
    s = s8 * SUBLANES
    assert nt * gp == s8 and nc * c == d

    def wspec(piece):
        return pl.BlockSpec((d, c), lambda i, j: (0, piece * nc + j))

    def bspec(piece):
        return pl.BlockSpec((1, c), lambda i, j: (0, piece * nc + j))

    def row(width):
        return pl.BlockSpec((1, width), lambda i, j: (0, 0))

    pieces = (0, 1, 2, 4, 5, 6)
    chunk_row = pl.BlockSpec((1, c), lambda i, j: (0, j))
    chunk_8 = pl.BlockSpec((SUBLANES, c), lambda i, j: (0, j))
    tok3 = pl.BlockSpec((gp, SUBLANES, c), lambda i, j: (i, 0, j))
    tok2 = pl.BlockSpec((t, c), lambda i, j: (i, j))
    in_specs = (
        [pl.BlockSpec((gp, SUBLANES, d), lambda i, j: (i, 0, 0)),
         pl.BlockSpec((SUBLANES, d), lambda i, j: (0, 0)),
         pl.BlockSpec((SUBLANES, d), lambda i, j: (0, 1)),
         row(d)]
        + [wspec(k) for k in pieces]
        + [bspec(k) for k in pieces]
        + [pl.BlockSpec((CONV_W, SUBLANES, c), lambda i, j: (0, 0, j)), chunk_row,
           tok3, tok3,
           pl.BlockSpec((SHORT, SUBLANES, c), lambda i, j: (0, 0, j)), chunk_8,
           pl.BlockSpec((c // RNN_BW, RNN_BW, 2 * RNN_BW), lambda i, j: (j, 0, 0)),
           chunk_row, chunk_row, chunk_row, chunk_8])
    out_specs = [tok3, tok2, tok2, tok2, tok2]
    out_shape = [jax.ShapeDtypeStruct((s8, SUBLANES, d), F32)] + [jax.ShapeDtypeStruct((s, d), BF16)] * 4
    scratch = [
        pltpu.VMEM((t, d), BF16),
        pltpu.VMEM((groups, p + 2 * CONV_PAD, SUBLANES, c), F32),
        pltpu.VMEM((groups, p, SUBLANES, c), F32),
        pltpu.VMEM((groups, p, SUBLANES, c), F32),
        pltpu.VMEM((gp, SUBLANES, c), F32),
        pltpu.VMEM((nc, SHORT - 1, SUBLANES, c), F32),
        pltpu.VMEM((nc, SUBLANES, c), F32),
    ]
    args = ([x3, mod, mod, norm_g] + [w_in] * len(pieces) + [b_in] * len(pieces)
            + [dw, dwb, xr, hb, cw, cb, wri, br, bi, lam, h0])
    return pl.pallas_call(
        functools.partial(_mixer_kernel, p=p),
        grid=(nt, nc),
        in_specs=in_specs,
        out_specs=out_specs,
        out_shape=out_shape,
        scratch_shapes=scratch,
        compiler_params=pltpu.CompilerParams(dimension_semantics=("arbitrary", "arbitrary"),
                                             vmem_limit_bytes=VMEM_LIMIT_BYTES),
        name="mixer",
    )(*args)


def _out_kernel(cv_ref, szc_ref, lr_ref, sgc_ref, sgr_ref, x_ref, gate_ref, lng_ref, lnb_ref, fg_ref,
                wc_ref, wr_ref, wo_ref, o_ref):
    yr = jnp.dot(lr_ref[...], wr_ref[...], preferred_element_type=F32)
    cv = cv_ref[...]
    mu = jnp.mean(cv, axis=-1, keepdims=True)
    dev = cv - mu
    var = jnp.mean(dev * dev, axis=-1, keepdims=True)
    y = dev * lax.rsqrt(var + EPS) * lng_ref[...] + lnb_ref[...]
    lc = (_silu(y) * szc_ref[...].astype(F32)).astype(BF16)
    yc = jnp.dot(lc, wc_ref[...], preferred_element_type=F32)
    y = (sgc_ref[...].astype(F32) * yc + sgr_ref[...].astype(F32) * yr).astype(BF16)
    o = jnp.dot(y, wo_ref[...], preferred_element_type=F32)
    xn = x_ref[...] + gate_ref[0:1, :] * o
    ms = jnp.mean(xn * xn, axis=-1, keepdims=True)
    o_ref[...] = xn * lax.rsqrt(ms + EPS) * fg_ref[...]


def _out_proj(cv, szc, lr, sgc, sgr, x2, mod, ln_g, ln_b, final_g, wc, wr, wo, *, t):
    s, d = x2.shape
    tok = pl.BlockSpec((t, d), lambda i: (i, 0))
    row = pl.BlockSpec((1, d), lambda i: (0, 0))
    resident = pl.BlockSpec((d, d), lambda i: (0, 0), pipeline_mode=pl.Buffered(1))
    return pl.pallas_call(
        _out_kernel,
        grid=(s // t,),
        in_specs=[tok, tok, tok, tok, tok, tok,
                  pl.BlockSpec((SUBLANES, d), lambda i: (0, 2)),
                  row, row, row,
                  resident, resident, resident],
        out_specs=tok,
        out_shape=jax.ShapeDtypeStruct((s, d), F32),
        compiler_params=pltpu.CompilerParams(dimension_semantics=("arbitrary",),
                                             vmem_limit_bytes=VMEM_LIMIT_BYTES),
        name="out_proj",
    )(cv, szc, lr, sgc, sgr, x2, mod, ln_g, ln_b, final_g, wc, wr, wo)


def _permute_tokens(x2, p):
    s, d = x2.shape
    nt = s // (p * SUBLANES)
    return x2.reshape(nt, SUBLANES, p, d).transpose(0, 2, 1, 3).reshape(s // SUBLANES, SUBLANES, d)


def _unpermute_tokens(y2, p):
    s, d = y2.shape
    nt = s // (p * SUBLANES)
    return y2.reshape(nt, p, SUBLANES, d).transpose(0, 2, 1, 3).reshape(s, d)


def kernel(x, c, ctx, c_ctx, w_ada, b_ada, norm_g, w_in, b_in, conv_dw, conv_dw_b, conv_ln_g, conv_ln_b, w_conv_out, rnn_conv, rnn_conv_b, rnn_w_r, rnn_b_r, rnn_w_i, rnn_b_i, rnn_lam, w_rnn_out, w_o, final_g):
    bsz, seq, d = x.shape
    ctx_len = ctx.shape[1]
    depth = w_in.shape[0]
    assert bsz == 1 and depth == 1
    assert w_in.shape[2] == 7 * d and rnn_w_r.shape[3] == RNN_BW and conv_dw.shape[1] == CONV_W
    assert seq % (SUBLANES * GRID_W) == 0 and ctx_len % SUBLANES == 0 and d % LANES == 0
    l = 0
    chunk = min(d, 256)
    p_ctx = ctx_len // SUBLANES

    w_in_b = w_in[l].astype(BF16)
    b_in2 = b_in[l][None]
    w_xr = w_in_b[:, 3 * d:4 * d]
    b_xr = b_in2[:, 3 * d:4 * d]
    wri = jnp.concatenate([rnn_w_r[l], rnn_w_i[l]], axis=-1).astype(BF16)
    cw = jnp.broadcast_to(rnn_conv[l][:, :, None, :], (2, SHORT, SUBLANES, d))
    cb = jnp.broadcast_to(rnn_conv_b[l][:, None, :], (2, SUBLANES, d))
    br = rnn_b_r[l][:, None, :]
    bi = rnn_b_i[l][:, None, :]
    lam = rnn_lam[l][:, None, :]
    dw = jnp.broadcast_to(conv_dw[l][:, None, :], (CONV_W, SUBLANES, d))
    norm_g2 = norm_g[l][None]

    cond = jnp.stack([c[0], c_ctx])
    mod = _adaln(jnp.broadcast_to(cond[:, :, None], (2, d, LANES)), w_ada[l], b_ada[l][None])

    def rnn_args(direction):
        return (cw[direction], cb[direction], wri[direction], br[direction], bi[direction],
                lam[direction])

    ctx3 = _permute_tokens(ctx[0], p_ctx)
    zeros8 = jnp.zeros((SUBLANES, d), F32)
    h0 = [_rnn_tiles(ctx3, mod, norm_g2, w_xr, b_xr, *rnn_args(direction), zeros8,
                     p=p_ctx, c=chunk, reverse=bool(direction), mod_row=1, write_states=False)
          for direction in (0, 1)]

    x3 = _permute_tokens(x[0], GRID_W)
    xr, hb = _rnn_tiles(x3, mod, norm_g2, w_xr, b_xr, *rnn_args(1), h0[1],
                        p=GRID_W, c=min(d, 512), reverse=True, mod_row=0, write_states=True)
    groups = 2 if seq % (2 * SUBLANES * GRID_W) == 0 else 1
    cv, szc, lr, sgc, sgr = _mixer(x3, mod, norm_g2, w_in_b, b_in2, dw, conv_dw_b[l][None], xr, hb,
                                   *rnn_args(0), h0[0], p=GRID_W, groups=groups, c=chunk)
    out = _out_proj(cv.reshape(seq, d), szc, lr, sgc, sgr, x3.reshape(seq, d), mod,
                    conv_ln_g[l][None], conv_ln_b[l][None], final_g[None],
                    w_conv_out[l].astype(BF16), w_rnn_out[l].astype(BF16), w_o[l].astype(BF16),
                    t=min(seq, 256))
    return _unpermute_tokens(out, GRID_W)[None]
```

```python
import functools
import math

import jax
import jax.numpy as jnp
from jax import lax
from jax.experimental import pallas as pl
from jax.experimental.pallas import tpu as pltpu

EPS = 1e-6
LRU_C = 8.0
GRID_W = 64
CONV_W = 31
CONV_HALF = CONV_W // 2
CONV_PAD = 16
CONV_BLOCK = 8
SHORT = 4
RNN_BW = 128

SUBLANES = 8
LANES = 128
VMEM_LIMIT_BYTES = 56 * 1024 * 1024

F32 = jnp.float32
BF16 = jnp.bfloat16


def _sigmoid(x):
    return jax.nn.sigmoid(x)


def _silu(x):
    return x * jax.nn.sigmoid(x)


def _always(i, k):
    return i + k >= 0


def _zero_from(x):
    u = pltpu.bitcast(x, jnp.uint32)
    z = lax.shift_right_logical(lax.shift_right_logical(u, jnp.uint32(16)), jnp.uint32(16))
    return pltpu.bitcast(z, F32)


def _rms_mod(x, g, scale, shift):
    ms = jnp.mean(x * x, axis=-1, keepdims=True)
    y = x * lax.rsqrt(ms + EPS)
    return y * (g * (1.0 + scale)) + shift


def _adaln_kernel(cb_ref, w_ref, b_ref, o_ref, acc_s):
    j = pl.program_id(0)
    bk, n = w_ref.shape
    rows = cb_ref.shape[0]

    @pl.when(j == 0)
    def _():
        acc_s[...] = jnp.zeros(acc_s.shape, F32)

    for r in range(rows):
        s3 = _silu(cb_ref[r]).reshape(bk // SUBLANES, SUBLANES, LANES)
        for nb in range(n // LANES):
            sl = slice(nb * LANES, (nb + 1) * LANES)
            w3 = w_ref[:, sl].reshape(bk // SUBLANES, SUBLANES, LANES)
            acc_s[r, :, sl] += jnp.sum(w3 * s3, axis=0)

    @pl.when(j == pl.num_programs(0) - 1)
    def _():
        o_ref[...] = jnp.zeros(o_ref.shape, F32)
        for r in range(rows):
            o_ref[r:r + 1, :] = jnp.sum(acc_s[r], axis=0, keepdims=True) + b_ref[...]


def _adaln(cb, w, b):
    rows, d, _ = cb.shape
    n = w.shape[1]
    bk = min(d, 128)
    assert d % bk == 0 and rows <= SUBLANES
    return pl.pallas_call(
        _adaln_kernel,
        grid=(d // bk,),
        in_specs=[pl.BlockSpec((rows, bk, LANES), lambda j: (0, j, 0)),
                  pl.BlockSpec((bk, n), lambda j: (j, 0)),
                  pl.BlockSpec((1, n), lambda j: (0, 0))],
        out_specs=pl.BlockSpec((SUBLANES, n), lambda j: (0, 0)),
        out_shape=jax.ShapeDtypeStruct((SUBLANES, n), F32),
        scratch_shapes=[pltpu.VMEM((rows, SUBLANES, n), F32)],
        compiler_params=pltpu.CompilerParams(dimension_semantics=("arbitrary",),
                                             vmem_limit_bytes=VMEM_LIMIT_BYTES),
        name="adaln",
    )(cb, w, b)


def _halo(xv, nbr, reverse, sub):
    p = xv.shape[0]
    out = []
    for j in range(SHORT - 1):
        if reverse:
            own = pltpu.roll(xv[j], SUBLANES - 1, axis=0)
            oth = pltpu.roll(nbr[j], SUBLANES - 1, axis=0)
            out.append(jnp.where(sub == SUBLANES - 1, oth, own))
        else:
            own = pltpu.roll(xv[p - (SHORT - 1) + j], 1, axis=0)
            oth = pltpu.roll(nbr[j], 1, axis=0)
            out.append(jnp.where(sub == 0, oth, own))
    return jnp.stack(out, axis=0)


def _short_conv(xv, halo, cw_ref, cb, reverse):
    p = xv.shape[0]
    xe = jnp.concatenate([xv, halo] if reverse else [halo, xv], axis=0)
    acc = cb[None] + cw_ref[0][None] * xe[0:p]
    for k in range(1, SHORT):
        acc = acc + cw_ref[k][None] * xe[k:k + p]
    return acc


def _gates_ab(xc, wri_ref, br, bi, lam, a_ref, b_ref):
    p, _, c = xc.shape
    x2 = xc.reshape(p * SUBLANES, c)
    xb = x2.astype(BF16)
    neg = -lam
    softplus = jnp.maximum(neg, 0.0) + jnp.log(1.0 + jnp.exp(-jnp.abs(neg)))
    cl2 = softplus * (-LRU_C * math.log2(math.e))
    for h in range(c // RNN_BW):
        sl = slice(h * RNN_BW, (h + 1) * RNN_BW)
        pre = jnp.dot(xb[:, sl], wri_ref[h], preferred_element_type=F32)
        r = _sigmoid(pre[:, :RNN_BW] + br[:, sl])
        i = _sigmoid(pre[:, RNN_BW:] + bi[:, sl])
        a = jnp.exp2(r * cl2[:, sl])
        om = 1.0 - a * a
        mult = jnp.where(om > 0.0, om * lax.rsqrt(om), 0.0)
        b = mult * (i * x2[:, sl])
        a_ref[:, :, sl] = a.reshape(p, SUBLANES, RNN_BW)
        b_ref[:, :, sl] = b.reshape(p, SUBLANES, RNN_BW)


def _lru_scan(a_ref, b_ref, carry_prev, reverse, h_ref):
    p, _, c = a_ref.shape
    sub = lax.broadcasted_iota(jnp.int32, (SUBLANES, c), 0)
    order = range(p - 1, -1, -1) if reverse else range(p)
    b_end = jnp.zeros((SUBLANES, c), F32)
    a_end = jnp.ones((SUBLANES, c), F32)
    for t in order:
        a = a_ref[t]
        b_end = a * b_end + b_ref[t]
        a_end = a_end * a
    for s in (1, 2, 4):
        shift = (SUBLANES - s) if reverse else s
        a_sh = pltpu.roll(a_end, shift, axis=0)
        b_sh = pltpu.roll(b_end, shift, axis=0)
        valid = (sub < SUBLANES - s) if reverse else (sub >= s)
        b_end = jnp.where(valid, a_end * b_sh + b_end, b_end)
        a_end = jnp.where(valid, a_end * a_sh, a_end)
    last = 0 if reverse else SUBLANES - 1
    first = SUBLANES - 1 - last
    c_in = jnp.broadcast_to(carry_prev[last:last + 1, :], (SUBLANES, c))
    h_out = b_end + a_end * c_in
    if h_ref is not None:
        h = jnp.where(sub == first, c_in,
                      pltpu.roll(h_out, (SUBLANES - 1) if reverse else 1, axis=0))
        for t in order:
            h = a_ref[t] * h + b_ref[t]
            h_ref[t] = h
    return h_out


def _rnn_gates(xv, nbr_ref, cw_ref, cb, wri_ref, br, bi, lam, a_s, b_s, reverse):
    p, _, c = xv.shape
    sub = lax.broadcasted_iota(jnp.int32, (SUBLANES, c), 0)
    halo = _halo(xv, nbr_ref[...], reverse, sub)
    nbr_ref[...] = xv[0:SHORT - 1] if reverse else xv[p - (SHORT - 1):p]
    xc = _short_conv(xv, halo, cw_ref, cb, reverse)
    _gates_ab(xc, wri_ref, br, bi, lam, a_s, b_s)


def _rnn_scan(carry_ref, a_s, b_s, reverse, h_ref):
    h_out = _lru_scan(a_s, b_s, carry_ref[...], reverse, h_ref)
    carry_ref[...] = h_out
    return h_out


def _rnn_tiles_kernel(x_ref, shift_ref, scale_ref, g_ref, w_ref, bx_ref, cw_ref, cb_ref, wri_ref,
                      br_ref, bi_ref, lam_ref, h0_ref, *rest, reverse, mod_row, write_states, c):
    if write_states:
        xr_ref, h_ref, hn_s, a_s, b_s, nbr_s, carry_s = rest
        hfin_ref = None
    else:
        hfin_ref, hn_s, xr_ref, a_s, b_s, nbr_s, carry_s = rest
        h_ref = None
    i = pl.program_id(0)
    p, _, d = x_ref.shape
    dr = w_ref.shape[1]
    nc = dr // c
    nb = c // RNN_BW

    @pl.when(i == 0)
    def _():
        nbr_s[...] = jnp.zeros(nbr_s.shape, F32)
        carry_s[...] = h0_ref[...]

    def lanes(k):
        return slice(k * c, (k + 1) * c)

    def project(k):
        xr = jnp.dot(hn_s[...], w_ref[:, lanes(k)], preferred_element_type=F32) + bx_ref[:, lanes(k)]
        xr_ref[:, :, lanes(k)] = xr.reshape(p, SUBLANES, c)

    def gates(k):
        sl = lanes(k)
        _rnn_gates(xr_ref[:, :, sl], nbr_s.at[:, :, sl], cw_ref.at[:, :, sl], cb_ref[:, sl],
                   wri_ref.at[k * nb:(k + 1) * nb], br_ref[:, sl], bi_ref[:, sl], lam_ref[:, sl],
                   a_s, b_s, reverse)

    def scan(k):
        sl = lanes(k)
        h_end = _rnn_scan(carry_s.at[:, sl], a_s, b_s, reverse,
                          None if h_ref is None else h_ref.at[:, :, sl])
        if hfin_ref is not None:
            hfin_ref[:, sl] = h_end

    x = x_ref[...].reshape(p * SUBLANES, d)
    row = slice(mod_row, mod_row + 1)
    hn_s[...] = _rms_mod(x, g_ref[...], scale_ref[row, :], shift_ref[row, :]).astype(BF16)
    project(0)
    for k in range(nc):

        @pl.when(_always(i, k))
        def _(k=k):
            gates(k)
            if k + 1 < nc:
                project(k + 1)
            scan(k)


def _rnn_tiles(x3, mod, norm_g, w_in, b_in, xr_col, cw, cb, wri, br, bi, lam, h0, *, p, c, reverse,
               mod_row, write_states):
    s8, _, d = x3.shape
    dr = cw.shape[-1]
    nt = s8 // p
    t = p * SUBLANES

    def tile(i):
        return (nt - 1 - i) if reverse else i

    def whole(shape):
        return pl.BlockSpec(shape, lambda i: (0,) * len(shape))

    in_specs = [
        pl.BlockSpec((p, SUBLANES, d), lambda i: (tile(i), 0, 0)),
        pl.BlockSpec((SUBLANES, d), lambda i: (0, 0)),
        pl.BlockSpec((SUBLANES, d), lambda i: (0, 1)),
        whole((1, d)),
        pl.BlockSpec((d, dr), lambda i: (0, xr_col), pipeline_mode=pl.Buffered(1)),
        pl.BlockSpec((1, dr), lambda i: (0, xr_col)),
        whole((SHORT, SUBLANES, dr)), whole((SUBLANES, dr)),
        whole((dr // RNN_BW, RNN_BW, 2 * RNN_BW)),
        whole((1, dr)), whole((1, dr)), whole((1, dr)),
        whole((SUBLANES, dr)),
    ]
    state = jax.ShapeDtypeStruct((s8, SUBLANES, dr), F32)
    if write_states:
        st_spec = pl.BlockSpec((p, SUBLANES, dr), lambda i: (tile(i), 0, 0))
        out_specs = [st_spec, st_spec]
        out_shape = [state, state]
        xr_scratch = []
    else:
        assert nt == 1
        out_specs = whole((SUBLANES, dr))
        out_shape = jax.ShapeDtypeStruct((SUBLANES, dr), F32)
        xr_scratch = [pltpu.VMEM((p, SUBLANES, dr), F32)]
    scratch = ([pltpu.VMEM((t, d), BF16)] + xr_scratch + [
        pltpu.VMEM((p, SUBLANES, c), F32),
        pltpu.VMEM((p, SUBLANES, c), F32),
        pltpu.VMEM((SHORT - 1, SUBLANES, dr), F32),
        pltpu.VMEM((SUBLANES, dr), F32),
    ])
    kern = functools.partial(_rnn_tiles_kernel, reverse=reverse, mod_row=mod_row,
                             write_states=write_states, c=c)
    return pl.pallas_call(
        kern,
        grid=(nt,),
        in_specs=in_specs,
        out_specs=out_specs,
        out_shape=out_shape,
        scratch_shapes=scratch,
        compiler_params=pltpu.CompilerParams(dimension_semantics=("arbitrary",),
                                             vmem_limit_bytes=VMEM_LIMIT_BYTES),
        name="rnn_rev" if reverse else "rnn_fwd",
    )(x3, mod, mod, norm_g, w_in, b_in, cw, cb, wri, br, bi, lam, h0)


def _mixer_kernel(x_ref, shift_ref, scale_ref, g_ref,
                  wa_ref, wg_ref, wzc_ref, wzr_ref, wgc_ref, wgr_ref,
                  ba_ref, bg_ref, bzc_ref, bzr_ref, bgc_ref, bgr_ref,
                  dw_ref, dwb_ref,
                  xr_ref, hb_ref, cw_ref, cb_ref, wri_ref, br_ref, bi_ref, lam_ref, h0_ref,
                  cv_ref, szc_ref, lr_ref, sgc_ref, sgr_ref,
                  hn_s, vpad_s, a_s, b_s, hf_s, nbr_s, carry_s, *, p):
    i = pl.program_id(0)
    j = pl.program_id(1)
    gp, _, d = x_ref.shape
    groups = gp // p
    c = wa_ref.shape[1]
    t = gp * SUBLANES

    @pl.when(j == 0)
    def _():
        x = x_ref[...].reshape(t, d)
        hn_s[...] = _rms_mod(x, g_ref[...], scale_ref[0:1, :], shift_ref[0:1, :]).astype(BF16)

    @pl.when((i == 0) & (j == 0))
    def _():
        zeros = jnp.zeros((CONV_PAD, SUBLANES, c), F32)
        for g in range(groups):
            vpad_s[g, 0:CONV_PAD] = zeros
            vpad_s[g, CONV_PAD + p:CONV_PAD + p + CONV_PAD] = zeros

    @pl.when(i == 0)
    def _():
        nbr_s[j] = jnp.zeros(nbr_s.shape[1:], F32)
        carry_s[j] = h0_ref[...]

    def proj(w_ref, b_ref):
        return jnp.dot(hn_s[...], w_ref[...], preferred_element_type=F32) + b_ref[...]

    for g in range(groups):
        _rnn_gates(xr_ref[g * p:(g + 1) * p], nbr_s.at[j], cw_ref, cb_ref[...], wri_ref, br_ref[...],
                   bi_ref[...], lam_ref[...], a_s.at[g], b_s.at[g], False)
    v = (proj(wa_ref, ba_ref) * _sigmoid(proj(wg_ref, bg_ref))).reshape(gp, SUBLANES, c)
    for g in range(groups):
        vpad_s[g, CONV_PAD:CONV_PAD + p] = v[g * p:(g + 1) * p]

    @pl.when(_always(i, 0))
    def _():
        off = CONV_PAD - CONV_HALF
        pb = CONV_BLOCK
        for g in range(groups):
            for q in range(p // pb):
                for lb in range(c // LANES):
                    sl = slice(lb * LANES, (lb + 1) * LANES)
                    acc = jnp.broadcast_to(dwb_ref[:, sl][None], (pb, SUBLANES, LANES))
                    for k in range(CONV_W):
                        lo_k = q * pb + k + off
                        acc = acc + dw_ref[k, :, sl][None] * vpad_s[g, lo_k:lo_k + pb, :, sl]
                    cv_ref[g * p + q * pb:g * p + (q + 1) * pb, :, sl] = acc
        szc_ref[...] = _silu(proj(wzc_ref, bzc_ref)).astype(BF16)
        for g in range(groups):
            _rnn_scan(carry_s.at[j], a_s.at[g], b_s.at[g], False, hf_s.at[g * p:(g + 1) * p])
        szr = _silu(proj(wzr_ref, bzr_ref))
        h = (hf_s[...] + hb_ref[...]).reshape(t, c)
        lr_ref[...] = (h * szr).astype(BF16)
        sgc_ref[...] = _sigmoid(proj(wgc_ref, bgc_ref)).astype(BF16)
        sgr_ref[...] = _sigmoid(proj(wgr_ref, bgr_ref)).astype(BF16)


def _mixer(x3, mod, norm_g, w_in, b_in, dw, dwb, xr, hb, cw, cb, wri, br, bi, lam, h0, *, p, groups, c):
    s8, _, d = x3.shape
    gp = groups * p
    nt = s8 // gp
    nc = d // c
    t = gp * SUBLANES
    s = s8 * SUBLANES
    assert nt * gp == s8 and nc * c == d

    def wspec(piece):
        return pl.BlockSpec((d, c), lambda i, j: (0, piece * nc + j))

    def bspec(piece):
        return pl.BlockSpec((1, c), lambda i, j: (0, piece * nc + j))

    def row(width):
        return pl.BlockSpec((1, width), lambda i, j: (0, 0))

    pieces = (0, 1, 2, 4, 5, 6)
    chunk_row = pl.BlockSpec((1, c), lambda i, j: (0, j))
    chunk_8 = pl.BlockSpec((SUBLANES, c), lambda i, j: (0, j))
    tok3 = pl.BlockSpec((gp, SUBLANES, c), lambda i, j: (i, 0, j))
    tok2 = pl.BlockSpec((t, c), lambda i, j: (i, j))
    in_specs = (
        [pl.BlockSpec((gp, SUBLANES, d), lambda i, j: (i, 0, 0)),
         pl.BlockSpec((SUBLANES, d), lambda i, j: (0, 0)),
         pl.BlockSpec((SUBLANES, d), lambda i, j: (0, 1)),
         row(d)]
        + [wspec(k) for k in pieces]
        + [bspec(k) for k in pieces]
        + [pl.BlockSpec((CONV_W, SUBLANES, c), lambda i, j: (0, 0, j)), chunk_row,
           tok3, tok3,
           pl.BlockSpec((SHORT, SUBLANES, c), lambda i, j: (0, 0, j)), chunk_8,
           pl.BlockSpec((c // RNN_BW, RNN_BW, 2 * RNN_BW), lambda i, j: (j, 0, 0)),
           chunk_row, chunk_row, chunk_row, chunk_8])
    out_specs = [tok3, tok2, tok2, tok2, tok2]
    out_shape = [jax.ShapeDtypeStruct((s8, SUBLANES, d), F32)] + [jax.ShapeDtypeStruct((s, d), BF16)] * 4
    scratch = [
        pltpu.VMEM((t, d), BF16),
        pltpu.VMEM((groups, p + 2 * CONV_PAD, SUBLANES, c), F32),
        pltpu.VMEM((groups, p, SUBLANES, c), F32),
        pltpu.VMEM((groups, p, SUBLANES, c), F32),
        pltpu.VMEM((gp, SUBLANES, c), F32),
        pltpu.VMEM((nc, SHORT - 1, SUBLANES, c), F32),
        pltpu.VMEM((nc, SUBLANES, c), F32),
    ]
    args = ([x3, mod, mod, norm_g] + [w_in] * len(pieces) + [b_in] * len(pieces)
            + [dw, dwb, xr, hb, cw, cb, wri, br, bi, lam, h0])
    return pl.pallas_call(
        functools.partial(_mixer_kernel, p=p),
        grid=(nt, nc),
        in_specs=in_specs,
        out_specs=out_specs,
        out_shape=out_shape,
        scratch_shapes=scratch,
        compiler_params=pltpu.CompilerParams(dimension_semantics=("arbitrary", "arbitrary"),
                                             vmem_limit_bytes=VMEM_LIMIT_BYTES),
        name="mixer",
    )(*args)


def _out_kernel(cv_ref, szc_ref, lr_ref, sgc_ref, sgr_ref, x_ref, gate_ref, lng_ref, lnb_ref, fg_ref,
                wc_ref, wr_ref, wo_ref, o_ref, res_s):
    yr = jnp.dot(lr_ref[...], wr_ref[...], preferred_element_type=F32)
    cv = cv_ref[...]
    mu = jnp.mean(cv, axis=-1, keepdims=True)
    dev = cv - mu
    var = jnp.mean(dev * dev, axis=-1, keepdims=True)
    y = dev * lax.rsqrt(var + EPS) * lng_ref[...] + lnb_ref[...]
    lc = (_silu(y) * szc_ref[...].astype(F32)).astype(BF16)
    yc = jnp.dot(lc, wc_ref[...], preferred_element_type=F32)
    y = (sgc_ref[...].astype(F32) * yc + sgr_ref[...].astype(F32) * yr).astype(BF16)
    o = jnp.dot(y, wo_ref[...], preferred_element_type=F32)
    xn = x_ref[...] + gate_ref[0:1, :] * o
    ms = jnp.mean(xn * xn, axis=-1, keepdims=True)
    res = xn * lax.rsqrt(ms + EPS) * fg_ref[...]
    t, d = res.shape
    for lb in range(d // LANES):
        res_s[lb] = res[:, lb * LANES:(lb + 1) * LANES]
    for g in range(SUBLANES):
        for lb in range(d // LANES):
            o_ref[g, :, lb * LANES:(lb + 1) * LANES] = res_s[lb, pl.ds(g, t // SUBLANES, stride=SUBLANES), :]


def _out_proj(cv, szc, lr, sgc, sgr, x2, mod, ln_g, ln_b, final_g, wc, wr, wo, *, t, p):
    s, d = x2.shape
    per_group = (p * SUBLANES) // t
    assert per_group * t == p * SUBLANES and t % (SUBLANES * SUBLANES) == 0
    tok = pl.BlockSpec((t, d), lambda i: (i, 0))
    row = pl.BlockSpec((1, d), lambda i: (0, 0))
    resident = pl.BlockSpec((d, d), lambda i: (0, 0), pipeline_mode=pl.Buffered(1))
    return pl.pallas_call(
        _out_kernel,
        grid=(s // t,),
        in_specs=[tok, tok, tok, tok, tok, tok,
                  pl.BlockSpec((SUBLANES, d), lambda i: (0, 2)),
                  row, row, row,
                  resident, resident, resident],
        out_specs=pl.BlockSpec((None, SUBLANES, t // SUBLANES, d),
                               lambda i: (i // per_group, 0, i % per_group, 0)),
        out_shape=jax.ShapeDtypeStruct((s // (p * SUBLANES), SUBLANES, p, d), F32),
        scratch_shapes=[pltpu.VMEM((d // LANES, t, LANES), F32)],
        compiler_params=pltpu.CompilerParams(dimension_semantics=("arbitrary",),
                                             vmem_limit_bytes=VMEM_LIMIT_BYTES),
        name="out_proj",
    )(cv, szc, lr, sgc, sgr, x2, mod, ln_g, ln_b, final_g, wc, wr, wo)


def _permute_tokens(x2, p):
    s, d = x2.shape
    nt = s // (p * SUBLANES)
    return x2.reshape(nt, SUBLANES, p, d).transpose(0, 2, 1, 3).reshape(s // SUBLANES, SUBLANES, d)


def kernel(x, c, ctx, c_ctx, w_ada, b_ada, norm_g, w_in, b_in, conv_dw, conv_dw_b, conv_ln_g, conv_ln_b, w_conv_out, rnn_conv, rnn_conv_b, rnn_w_r, rnn_b_r, rnn_w_i, rnn_b_i, rnn_lam, w_rnn_out, w_o, final_g):
    bsz, seq, d = x.shape
    ctx_len = ctx.shape[1]
    depth = w_in.shape[0]
    assert bsz == 1 and depth == 1
    assert w_in.shape[2] == 7 * d and rnn_w_r.shape[3] == RNN_BW and conv_dw.shape[1] == CONV_W
    assert seq % (SUBLANES * GRID_W) == 0 and ctx_len % SUBLANES == 0 and d % LANES == 0
    l = 0
    chunk = min(d, 256)
    p_ctx = ctx_len // SUBLANES

    w_in_b = w_in[l].astype(BF16)
    b_in2 = b_in[l][None]
    wri = jnp.concatenate([rnn_w_r[l], rnn_w_i[l]], axis=-1).astype(BF16)
    cw = jnp.broadcast_to(rnn_conv[l][:, :, None, :], (2, SHORT, SUBLANES, d))
    cb = jnp.broadcast_to(rnn_conv_b[l][:, None, :], (2, SUBLANES, d))
    br = rnn_b_r[l][:, None, :]
    bi = rnn_b_i[l][:, None, :]
    lam = rnn_lam[l][:, None, :]
    dw = jnp.broadcast_to(conv_dw[l][:, None, :], (CONV_W, SUBLANES, d))
    norm_g2 = norm_g[l][None]

    cond = jnp.stack([c[0], c_ctx])
    mod = _adaln(jnp.broadcast_to(cond[:, :, None], (2, d, LANES)), w_ada[l], b_ada[l][None])

    def rnn_args(direction):
        return (cw[direction], cb[direction], wri[direction], br[direction], bi[direction],
                lam[direction])

    ctx3 = _permute_tokens(ctx[0], p_ctx)
    zeros8 = jnp.zeros((SUBLANES, d), F32)
    h0 = [_rnn_tiles(ctx3, mod, norm_g2, w_in_b, b_in2, 3, *rnn_args(direction), zeros8,
                     p=p_ctx, c=chunk, reverse=bool(direction), mod_row=1, write_states=False)
          for direction in (0, 1)]

    x3 = _permute_tokens(x[0], GRID_W)
    xr, hb = _rnn_tiles(x3, mod, norm_g2, w_in_b, b_in2, 3, *rnn_args(1), h0[1],
                        p=GRID_W, c=min(d, 512), reverse=True, mod_row=0, write_states=True)
    groups = 2 if seq % (2 * SUBLANES * GRID_W) == 0 else 1
    cv, szc, lr, sgc, sgr = _mixer(x3, mod, norm_g2, w_in_b, b_in2, dw, conv_dw_b[l][None], xr, hb,
                                   *rnn_args(0), h0[0], p=GRID_W, groups=groups, c=chunk)
    out = _out_proj(cv.reshape(seq, d), szc, lr, sgc, sgr, x3.reshape(seq, d), mod,
                    conv_ln_g[l][None], conv_ln_b[l][None], final_g[None],
                    w_conv_out[l].astype(BF16), w_rnn_out[l].astype(BF16), w_o[l].astype(BF16),
                    t=min(seq, 256), p=GRID_W)
    return out.reshape(1, seq, d)
```

```python
import functools
import math

import jax
import jax.numpy as jnp
from jax import lax
from jax.experimental import pallas as pl
from jax.experimental.pallas import tpu as pltpu

EPS = 1e-6
LRU_C = 8.0
GRID_W = 64
CONV_W = 31
CONV_HALF = CONV_W // 2
CONV_PAD = 16
CONV_BLOCK = 8
SHORT = 4
RNN_BW = 128

SUBLANES = 8
LANES = 128
VMEM_LIMIT_BYTES = 56 * 1024 * 1024

F32 = jnp.float32
BF16 = jnp.bfloat16


def _sigmoid(x):
    return jax.nn.sigmoid(x)


def _silu(x):
    return x * jax.nn.sigmoid(x)


def _always(i, k):
    return i + k >= 0


def _rms_mod(x, g, scale, shift):
    ms = jnp.mean(x * x, axis=-1, keepdims=True)
    y = x * lax.rsqrt(ms + EPS)
    return y * (g * (1.0 + scale)) + shift


def _adaln_kernel(cb_ref, w_ref, b_ref, o_ref, acc_s):
    j = pl.program_id(0)
    bk, n = w_ref.shape
    rows = cb_ref.shape[0]

    @pl.when(j == 0)
    def _():
        acc_s[...] = jnp.zeros(acc_s.shape, F32)

    for r in range(rows):
        s3 = _silu(cb_ref[r]).reshape(bk // SUBLANES, SUBLANES, LANES)
        for nb in range(n // LANES):
            sl = slice(nb * LANES, (nb + 1) * LANES)
            w3 = w_ref[:, sl].reshape(bk // SUBLANES, SUBLANES, LANES)
            acc_s[r, :, sl] += jnp.sum(w3 * s3, axis=0)

    @pl.when(j == pl.num_programs(0) - 1)
    def _():
        o_ref[...] = jnp.zeros(o_ref.shape, F32)
        for r in range(rows):
            o_ref[r:r + 1, :] = jnp.sum(acc_s[r], axis=0, keepdims=True) + b_ref[...]


def _adaln(cb, w, b):
    rows, d, _ = cb.shape
    n = w.shape[1]
    bk = min(d, 128)
    assert d % bk == 0 and rows <= SUBLANES
    return pl.pallas_call(
        _adaln_kernel,
        grid=(d // bk,),
        in_specs=[pl.BlockSpec((rows, bk, LANES), lambda j: (0, j, 0)),
                  pl.BlockSpec((bk, n), lambda j: (j, 0)),
                  pl.BlockSpec((1, n), lambda j: (0, 0))],
        out_specs=pl.BlockSpec((SUBLANES, n), lambda j: (0, 0)),
        out_shape=jax.ShapeDtypeStruct((SUBLANES, n), F32),
        scratch_shapes=[pltpu.VMEM((rows, SUBLANES, n), F32)],
        compiler_params=pltpu.CompilerParams(dimension_semantics=("arbitrary",),
                                             vmem_limit_bytes=VMEM_LIMIT_BYTES),
        name="adaln",
    )(cb, w, b)


def _halo(xv, nbr, reverse, sub):
    p = xv.shape[0]
    out = []
    for j in range(SHORT - 1):
        if reverse:
            own = pltpu.roll(xv[j], SUBLANES - 1, axis=0)
            oth = pltpu.roll(nbr[j], SUBLANES - 1, axis=0)
            out.append(jnp.where(sub == SUBLANES - 1, oth, own))
        else:
            own = pltpu.roll(xv[p - (SHORT - 1) + j], 1, axis=0)
            oth = pltpu.roll(nbr[j], 1, axis=0)
            out.append(jnp.where(sub == 0, oth, own))
    return jnp.stack(out, axis=0)


def _short_conv(xv, halo, cw_ref, cb, reverse):
    p = xv.shape[0]
    xe = jnp.concatenate([xv, halo] if reverse else [halo, xv], axis=0)
    acc = cb[None] + cw_ref[0][None] * xe[0:p]
    for k in range(1, SHORT):
        acc = acc + cw_ref[k][None] * xe[k:k + p]
    return acc


def _gates_ab(xc, wri_ref, br, bi, lam, a_ref, b_ref):
    p, _, c = xc.shape
    x2 = xc.reshape(p * SUBLANES, c)
    xb = x2.astype(BF16)
    neg = -lam
    softplus = jnp.maximum(neg, 0.0) + jnp.log(1.0 + jnp.exp(-jnp.abs(neg)))
    cl2 = softplus * (-LRU_C * math.log2(math.e))
    for h in range(c // RNN_BW):
        sl = slice(h * RNN_BW, (h + 1) * RNN_BW)
        pre = jnp.dot(xb[:, sl], wri_ref[h], preferred_element_type=F32)
        r = _sigmoid(pre[:, :RNN_BW] + br[:, sl])
        i = _sigmoid(pre[:, RNN_BW:] + bi[:, sl])
        a = jnp.exp2(r * cl2[:, sl])
        om = 1.0 - a * a
        mult = jnp.where(om > 0.0, om * lax.rsqrt(om), 0.0)
        b = mult * (i * x2[:, sl])
        a_ref[:, :, sl] = a.reshape(p, SUBLANES, RNN_BW)
        b_ref[:, :, sl] = b.reshape(p, SUBLANES, RNN_BW)


def _lru_scan(a_ref, b_ref, carry_prev, reverse, h_ref):
    p, _, c = a_ref.shape
    sub = lax.broadcasted_iota(jnp.int32, (SUBLANES, c), 0)
    order = range(p - 1, -1, -1) if reverse else range(p)
    b_end = jnp.zeros((SUBLANES, c), F32)
    a_end = jnp.ones((SUBLANES, c), F32)
    for t in order:
        a = a_ref[t]
        b_end = a * b_end + b_ref[t]
        a_end = a_end * a
    for s in (1, 2, 4):
        shift = (SUBLANES - s) if reverse else s
        a_sh = pltpu.roll(a_end, shift, axis=0)
        b_sh = pltpu.roll(b_end, shift, axis=0)
        valid = (sub < SUBLANES - s) if reverse else (sub >= s)
        b_end = jnp.where(valid, a_end * b_sh + b_end, b_end)
        a_end = jnp.where(valid, a_end * a_sh, a_end)
    last = 0 if reverse else SUBLANES - 1
    first = SUBLANES - 1 - last
    c_in = jnp.broadcast_to(carry_prev[last:last + 1, :], (SUBLANES, c))
    h_out = b_end + a_end * c_in
    if h_ref is not None:
        h = jnp.where(sub == first, c_in,
                      pltpu.roll(h_out, (SUBLANES - 1) if reverse else 1, axis=0))
        for t in order:
            h = a_ref[t] * h + b_ref[t]
            h_ref[t] = h
    return h_out


def _rnn_gates(xr_ref, groups, nbr_ref, cw_ref, cb, wri_ref, br, bi, lam, a_s, b_s, reverse):
    gp, _, c = xr_ref.shape
    p = gp // groups
    sub = lax.broadcasted_iota(jnp.int32, (SUBLANES, c), 0)
    xcs = [None] * groups
    for g in (range(groups - 1, -1, -1) if reverse else range(groups)):
        xv = xr_ref[g * p:(g + 1) * p]
        halo = _halo(xv, nbr_ref[...], reverse, sub)
        nbr_ref[...] = xv[0:SHORT - 1] if reverse else xv[p - (SHORT - 1):p]
        xcs[g] = _short_conv(xv, halo, cw_ref, cb, reverse)
    xc = xcs[0] if groups == 1 else jnp.concatenate(xcs, axis=0)
    _gates_ab(xc, wri_ref, br, bi, lam, a_s, b_s)


def _rnn_scan(carry_ref, a_s, b_s, reverse, h_ref):
    h_out = _lru_scan(a_s, b_s, carry_ref[...], reverse, h_ref)
    carry_ref[...] = h_out
    return h_out


def _rnn_tiles_kernel(x_ref, shift_ref, scale_ref, g_ref, w_ref, bx_ref, cw_ref, cb_ref, wri_ref,
                      br_ref, bi_ref, lam_ref, h0_ref, *rest, reverse, mod_row, write_states, c):
    if write_states:
        xr_ref, h_ref, hn_s, a_s, b_s, nbr_s, carry_s = rest
        hfin_ref = None
    else:
        hfin_ref, hn_s, xr_ref, a_s, b_s, nbr_s, carry_s = rest
        h_ref = None
    i = pl.program_id(0)
    p, _, d = x_ref.shape
    dr = w_ref.shape[1]
    nc = dr // c
    nb = c // RNN_BW

    @pl.when(i == 0)
    def _():
        nbr_s[...] = jnp.zeros(nbr_s.shape, F32)
        carry_s[...] = h0_ref[...]

    def lanes(k):
        return slice(k * c, (k + 1) * c)

    def project(k):
        xr = jnp.dot(hn_s[...], w_ref[:, lanes(k)], preferred_element_type=F32) + bx_ref[:, lanes(k)]
        xr_ref[:, :, lanes(k)] = xr.reshape(p, SUBLANES, c)

    def gates(k):
        sl = lanes(k)
        _rnn_gates(xr_ref.at[:, :, sl], 1, nbr_s.at[:, :, sl], cw_ref.at[:, :, sl], cb_ref[:, sl],
                   wri_ref.at[k * nb:(k + 1) * nb], br_ref[:, sl], bi_ref[:, sl], lam_ref[:, sl],
                   a_s, b_s, reverse)

    def scan(k):
        sl = lanes(k)
        h_end = _rnn_scan(carry_s.at[:, sl], a_s, b_s, reverse,
                          None if h_ref is None else h_ref.at[:, :, sl])
        if hfin_ref is not None:
            hfin_ref[:, sl] = h_end

    x = x_ref[...].reshape(p * SUBLANES, d)
    row = slice(mod_row, mod_row + 1)
    hn_s[...] = _rms_mod(x, g_ref[...], scale_ref[row, :], shift_ref[row, :]).astype(BF16)
    project(0)
    for k in range(nc):

        @pl.when(_always(i, k))
        def _(k=k):
            gates(k)
            if k + 1 < nc:
                project(k + 1)
            scan(k)


def _rnn_tiles(x3, mod, norm_g, w_in, b_in, xr_col, cw, cb, wri, br, bi, lam, h0, *, p, c, reverse,
               mod_row, write_states):
    s8, _, d = x3.shape
    dr = cw.shape[-1]
    nt = s8 // p
    t = p * SUBLANES

    def tile(i):
        return (nt - 1 - i) if reverse else i

    def whole(shape):
        return pl.BlockSpec(shape, lambda i: (0,) * len(shape))

    in_specs = [
        pl.BlockSpec((p, SUBLANES, d), lambda i: (tile(i), 0, 0)),
        pl.BlockSpec((SUBLANES, d), lambda i: (0, 0)),
        pl.BlockSpec((SUBLANES, d), lambda i: (0, 1)),
        whole((1, d)),
        pl.BlockSpec((d, dr), lambda i: (0, xr_col), pipeline_mode=pl.Buffered(1)),
        pl.BlockSpec((1, dr), lambda i: (0, xr_col)),
        whole((SHORT, SUBLANES, dr)), whole((SUBLANES, dr)),
        whole((dr // RNN_BW, RNN_BW, 2 * RNN_BW)),
        whole((1, dr)), whole((1, dr)), whole((1, dr)),
        whole((SUBLANES, dr)),
    ]
    state = jax.ShapeDtypeStruct((s8, SUBLANES, dr), F32)
    if write_states:
        st_spec = pl.BlockSpec((p, SUBLANES, dr), lambda i: (tile(i), 0, 0))
        out_specs = [st_spec, st_spec, pl.BlockSpec((t, d), lambda i: (tile(i), 0))]
        out_shape = [state, state, jax.ShapeDtypeStruct((s8 * SUBLANES, d), BF16)]
        own_scratch = []
    else:
        assert nt == 1
        out_specs = whole((SUBLANES, dr))
        out_shape = jax.ShapeDtypeStruct((SUBLANES, dr), F32)
        own_scratch = [pltpu.VMEM((t, d), BF16), pltpu.VMEM((p, SUBLANES, dr), F32)]
    scratch = (own_scratch + [
        pltpu.VMEM((p, SUBLANES, c), F32),
        pltpu.VMEM((p, SUBLANES, c), F32),
        pltpu.VMEM((SHORT - 1, SUBLANES, dr), F32),
        pltpu.VMEM((SUBLANES, dr), F32),
    ])
    kern = functools.partial(_rnn_tiles_kernel, reverse=reverse, mod_row=mod_row,
                             write_states=write_states, c=c)
    return pl.pallas_call(
        kern,
        grid=(nt,),
        in_specs=in_specs,
        out_specs=out_specs,
        out_shape=out_shape,
        scratch_shapes=scratch,
        compiler_params=pltpu.CompilerParams(dimension_semantics=("arbitrary",),
                                             vmem_limit_bytes=VMEM_LIMIT_BYTES),
        name="rnn_rev" if reverse else "rnn_fwd",
    )(x3, mod, mod, norm_g, w_in, b_in, cw, cb, wri, br, bi, lam, h0)


def _mixer_kernel(hn_ref,
                  wa_ref, wg_ref, wzc_ref, wzr_ref, wgc_ref, wgr_ref,
                  ba_ref, bg_ref, bzc_ref, bzr_ref, bgc_ref, bgr_ref,
                  dw_ref, dwb_ref,
                  xr_ref, hb_ref, cw_ref, cb_ref, wri_ref, br_ref, bi_ref, lam_ref, h0_ref,
                  cv_ref, szc_ref, lr_ref, sgc_ref, sgr_ref,
                  hn_s, vpad_s, a_s, b_s, hf_s, nbr_s, carry_s, *, p):
    i = pl.program_id(0)
    j = pl.program_id(1)
    gp = xr_ref.shape[0]
    groups = gp // p
    c = wa_ref.shape[1]
    t = gp * SUBLANES

    @pl.when(j == 0)
    def _():
        hn_s[...] = hn_ref[...]

    @pl.when((i == 0) & (j == 0))
    def _():
        zeros = jnp.zeros((CONV_PAD, SUBLANES, c), F32)
        for g in range(groups):
            vpad_s[g, 0:CONV_PAD] = zeros
            vpad_s[g, CONV_PAD + p:CONV_PAD + p + CONV_PAD] = zeros

    @pl.when(i == 0)
    def _():
        nbr_s[j] = jnp.zeros(nbr_s.shape[1:], F32)
        carry_s[j] = h0_ref[...]

    def proj(w_ref, b_ref):
        return jnp.dot(hn_s[...], w_ref[...], preferred_element_type=F32) + b_ref[...]

    _rnn_gates(xr_ref, groups, nbr_s.at[j], cw_ref, cb_ref[...], wri_ref, br_ref[...], bi_ref[...],
               lam_ref[...], a_s, b_s, False)
    v = (proj(wa_ref, ba_ref) * _sigmoid(proj(wg_ref, bg_ref))).reshape(gp, SUBLANES, c)
    for g in range(groups):
        vpad_s[g, CONV_PAD:CONV_PAD + p] = v[g * p:(g + 1) * p]

    @pl.when(_always(i, 0))
    def _():
        off = CONV_PAD - CONV_HALF
        pb = CONV_BLOCK
        for g in range(groups):
            for q in range(p // pb):
                for lb in range(c // LANES):
                    sl = slice(lb * LANES, (lb + 1) * LANES)
                    acc = jnp.broadcast_to(dwb_ref[:, sl][None], (pb, SUBLANES, LANES))
                    for k in range(CONV_W):
                        lo_k = q * pb + k + off
                        acc = acc + dw_ref[k, :, sl][None] * vpad_s[g, lo_k:lo_k + pb, :, sl]
                    cv_ref[g * p + q * pb:g * p + (q + 1) * pb, :, sl] = acc
        szc_ref[...] = _silu(proj(wzc_ref, bzc_ref)).astype(BF16)
        for g in range(groups):
            rows = slice(g * p, (g + 1) * p)
            _rnn_scan(carry_s.at[j], a_s.at[rows], b_s.at[rows], False, hf_s.at[rows])
        szr = _silu(proj(wzr_ref, bzr_ref))
        h = (hf_s[...] + hb_ref[...]).reshape(t, c)
        lr_ref[...] = (h * szr).astype(BF16)
        sgc_ref[...] = _sigmoid(proj(wgc_ref, bgc_ref)).astype(BF16)
        sgr_ref[...] = _sigmoid(proj(wgr_ref, bgr_ref)).astype(BF16)


def _mixer(hn, w_in, b_in, dw, dwb, xr, hb, cw, cb, wri, br, bi, lam, h0, *, p, groups, c):
    s, d = hn.shape
    s8 = s // SUBLANES
    gp = groups * p
    nt = s8 // gp
    nc = d // c
    t = gp * SUBLANES
    assert nt * gp == s8 and nc * c == d

    def wspec(piece):
        return pl.BlockSpec((d, c), lambda i, j: (0, piece * nc + j))

    def bspec(piece):
        return pl.BlockSpec((1, c), lambda i, j: (0, piece * nc + j))

    pieces = (0, 1, 2, 4, 5, 6)
    chunk_row = pl.BlockSpec((1, c), lambda i, j: (0, j))
    chunk_8 = pl.BlockSpec((SUBLANES, c), lambda i, j: (0, j))
    tok3 = pl.BlockSpec((gp, SUBLANES, c), lambda i, j: (i, 0, j))
    tok2 = pl.BlockSpec((t, c), lambda i, j: (i, j))
    in_specs = (
        [pl.BlockSpec((t, d), lambda i, j: (i, 0))]
        + [wspec(k) for k in pieces]
        + [bspec(k) for k in pieces]
        + [pl.BlockSpec((CONV_W, SUBLANES, c), lambda i, j: (0, 0, j)), chunk_row,
           tok3, tok3,
           pl.BlockSpec((SHORT, SUBLANES, c), lambda i, j: (0, 0, j)), chunk_8,
           pl.BlockSpec((c // RNN_BW, RNN_BW, 2 * RNN_BW), lambda i, j: (j, 0, 0)),
           chunk_row, chunk_row, chunk_row, chunk_8])
    out_specs = [tok3, tok2, tok2, tok2, tok2]
    out_shape = [jax.ShapeDtypeStruct((s8, SUBLANES, d), F32)] + [jax.ShapeDtypeStruct((s, d), BF16)] * 4
    scratch = [
        pltpu.VMEM((t, d), BF16),
        pltpu.VMEM((groups, p + 2 * CONV_PAD, SUBLANES, c), F32),
        pltpu.VMEM((gp, SUBLANES, c), F32),
        pltpu.VMEM((gp, SUBLANES, c), F32),
        pltpu.VMEM((gp, SUBLANES, c), F32),
        pltpu.VMEM((nc, SHORT - 1, SUBLANES, c), F32),
        pltpu.VMEM((nc, SUBLANES, c), F32),
    ]
    args = ([hn] + [w_in] * len(pieces) + [b_in] * len(pieces)
            + [dw, dwb, xr, hb, cw, cb, wri, br, bi, lam, h0])
    return pl.pallas_call(
        functools.partial(_mixer_kernel, p=p),
        grid=(nt, nc),
        in_specs=in_specs,
        out_specs=out_specs,
        out_shape=out_shape,
        scratch_shapes=scratch,
        compiler_params=pltpu.CompilerParams(dimension_semantics=("arbitrary", "arbitrary"),
                                             vmem_limit_bytes=VMEM_LIMIT_BYTES),
        name="mixer",
    )(*args)


def _out_kernel(cv_ref, szc_ref, lr_ref, sgc_ref, sgr_ref, x_ref, gate_ref, lng_ref, lnb_ref, fg_ref,
                wc_ref, wr_ref, wo_ref, o_ref, res_s):
    yr = jnp.dot(lr_ref[...], wr_ref[...], preferred_element_type=F32)
    cv = cv_ref[...]
    mu = jnp.mean(cv, axis=-1, keepdims=True)
    dev = cv - mu
    var = jnp.mean(dev * dev, axis=-1, keepdims=True)
    y = dev * lax.rsqrt(var + EPS) * lng_ref[...] + lnb_ref[...]
    lc = (_silu(y) * szc_ref[...].astype(F32)).astype(BF16)
    yc = jnp.dot(lc, wc_ref[...], preferred_element_type=F32)
    y = (sgc_ref[...].astype(F32) * yc + sgr_ref[...].astype(F32) * yr).astype(BF16)
    o = jnp.dot(y, wo_ref[...], preferred_element_type=F32)
    xn = x_ref[...] + gate_ref[0:1, :] * o
    ms = jnp.mean(xn * xn, axis=-1, keepdims=True)
    res = xn * lax.rsqrt(ms + EPS) * fg_ref[...]
    t, d = res.shape
    for lb in range(d // LANES):
        res_s[lb] = res[:, lb * LANES:(lb + 1) * LANES]
    for g in range(SUBLANES):
        for lb in range(d // LANES):
            o_ref[g, :, lb * LANES:(lb + 1) * LANES] = res_s[lb, pl.ds(g, t // SUBLANES, stride=SUBLANES), :]


def _out_proj(cv, szc, lr, sgc, sgr, x2, mod, ln_g, ln_b, final_g, wc, wr, wo, *, t, p):
    s, d = x2.shape
    per_group = (p * SUBLANES) // t
    assert per_group * t == p * SUBLANES and t % (SUBLANES * SUBLANES) == 0
    tok = pl.BlockSpec((t, d), lambda i: (i, 0))
    row = pl.BlockSpec((1, d), lambda i: (0, 0))
    resident = pl.BlockSpec((d, d), lambda i: (0, 0), pipeline_mode=pl.Buffered(1))
    return pl.pallas_call(
        _out_kernel,
        grid=(s // t,),
        in_specs=[tok, tok, tok, tok, tok, tok,
                  pl.BlockSpec((SUBLANES, d), lambda i: (0, 2)),
                  row, row, row,
                  resident, resident, resident],
        out_specs=pl.BlockSpec((None, SUBLANES, t // SUBLANES, d),
                               lambda i: (i // per_group, 0, i % per_group, 0)),
        out_shape=jax.ShapeDtypeStruct((s // (p * SUBLANES), SUBLANES, p, d), F32),
        scratch_shapes=[pltpu.VMEM((d // LANES, t, LANES), F32)],
        compiler_params=pltpu.CompilerParams(dimension_semantics=("arbitrary",),
                                             vmem_limit_bytes=VMEM_LIMIT_BYTES),
        name="out_proj",
    )(cv, szc, lr, sgc, sgr, x2, mod, ln_g, ln_b, final_g, wc, wr, wo)


def _permute_tokens(x2, p):
    s, d = x2.shape
    nt = s // (p * SUBLANES)
    return x2.reshape(nt, SUBLANES, p, d).transpose(0, 2, 1, 3).reshape(s // SUBLANES, SUBLANES, d)


def kernel(x, c, ctx, c_ctx, w_ada, b_ada, norm_g, w_in, b_in, conv_dw, conv_dw_b, conv_ln_g, conv_ln_b, w_conv_out, rnn_conv, rnn_conv_b, rnn_w_r, rnn_b_r, rnn_w_i, rnn_b_i, rnn_lam, w_rnn_out, w_o, final_g):
    bsz, seq, d = x.shape
    ctx_len = ctx.shape[1]
    depth = w_in.shape[0]
    assert bsz == 1 and depth == 1
    assert w_in.shape[2] == 7 * d and rnn_w_r.shape[3] == RNN_BW and conv_dw.shape[1] == CONV_W
    assert seq % (SUBLANES * GRID_W) == 0 and ctx_len % SUBLANES == 0 and d % LANES == 0
    l = 0
    chunk = min(d, 256)
    p_ctx = ctx_len // SUBLANES

    w_in_b = w_in[l].astype(BF16)
    b_in2 = b_in[l][None]
    wri = jnp.concatenate([rnn_w_r[l], rnn_w_i[l]], axis=-1).astype(BF16)
    cw = jnp.broadcast_to(rnn_conv[l][:, :, None, :], (2, SHORT, SUBLANES, d))
    cb = jnp.broadcast_to(rnn_conv_b[l][:, None, :], (2, SUBLANES, d))
    br = rnn_b_r[l][:, None, :]
    bi = rnn_b_i[l][:, None, :]
    lam = rnn_lam[l][:, None, :]
    dw = jnp.broadcast_to(conv_dw[l][:, None, :], (CONV_W, SUBLANES, d))
    norm_g2 = norm_g[l][None]

    cond = jnp.stack([c[0], c_ctx])
    mod = _adaln(jnp.broadcast_to(cond[:, :, None], (2, d, LANES)), w_ada[l], b_ada[l][None])

    def rnn_args(direction):
        return (cw[direction], cb[direction], wri[direction], br[direction], bi[direction],
                lam[direction])

    ctx3 = _permute_tokens(ctx[0], p_ctx)
    zeros8 = jnp.zeros((SUBLANES, d), F32)
    h0 = [_rnn_tiles(ctx3, mod, norm_g2, w_in_b, b_in2, 3, *rnn_args(direction), zeros8,
                     p=p_ctx, c=chunk, reverse=bool(direction), mod_row=1, write_states=False)
          for direction in (0, 1)]

    x3 = _permute_tokens(x[0], GRID_W)
    xr, hb, hn = _rnn_tiles(x3, mod, norm_g2, w_in_b, b_in2, 3, *rnn_args(1), h0[1],
                            p=GRID_W, c=min(d, 512), reverse=True, mod_row=0, write_states=True)
    groups = 2 if seq % (2 * SUBLANES * GRID_W) == 0 else 1
    cv, szc, lr, sgc, sgr = _mixer(hn, w_in_b, b_in2, dw, conv_dw_b[l][None], xr, hb,
                                   *rnn_args(0), h0[0], p=GRID_W, groups=groups, c=chunk)
    out = _out_proj(cv.reshape(seq, d), szc, lr, sgc, sgr, x3.reshape(seq, d), mod,
                    conv_ln_g[l][None], conv_ln_b[l][None], final_g[None],
                    w_conv_out[l].astype(BF16), w_rnn_out[l].astype(BF16), w_o[l].astype(BF16),
                    t=min(seq, 256), p=GRID_W)
    return out.reshape(1, seq, d)
```

```python
import functools
import math

import jax
import jax.numpy as jnp
from jax import lax
from jax.experimental import pallas as pl
from jax.experimental.pallas import tpu as pltpu

EPS = 1e-6
LRU_C = 8.0
GRID_W = 64
CONV_W = 31
CONV_HALF = CONV_W // 2
CONV_PAD = 16
CONV_BLOCK = 8
SHORT = 4
RNN_BW = 128

SUBLANES = 8
LANES = 128
VMEM_LIMIT_BYTES = 56 * 1024 * 1024

F32 = jnp.float32
BF16 = jnp.bfloat16


def _sigmoid(x):
    return jax.nn.sigmoid(x)


def _silu(x):
    return x * jax.nn.sigmoid(x)


def _always(i, k):
    return i + k >= 0


def _rms_mod(x, g, scale, shift):
    ms = jnp.mean(x * x, axis=-1, keepdims=True)
    y = x * lax.rsqrt(ms + EPS)
    return y * (g * (1.0 + scale)) + shift


def _adaln_kernel(cb_ref, w_ref, b_ref, o_ref, acc_s):
    j = pl.program_id(0)
    bk, n = w_ref.shape
    rows = cb_ref.shape[0]

    @pl.when(j == 0)
    def _():
        acc_s[...] = jnp.zeros(acc_s.shape, F32)

    for r in range(rows):
        s3 = _silu(cb_ref[r]).reshape(bk // SUBLANES, SUBLANES, LANES)
        for nb in range(n // LANES):
            sl = slice(nb * LANES, (nb + 1) * LANES)
            w3 = w_ref[:, sl].reshape(bk // SUBLANES, SUBLANES, LANES)
            acc_s[r, :, sl] += jnp.sum(w3 * s3, axis=0)

    @pl.when(j == pl.num_programs(0) - 1)
    def _():
        o_ref[...] = jnp.zeros(o_ref.shape, F32)
        for r in range(rows):
            o_ref[r:r + 1, :] = jnp.sum(acc_s[r], axis=0, keepdims=True) + b_ref[...]


def _adaln(cb, w, b):
    rows, d, _ = cb.shape
    n = w.shape[1]
    bk = min(d, 128)
    assert d % bk == 0 and rows <= SUBLANES
    return pl.pallas_call(
        _adaln_kernel,
        grid=(d // bk,),
        in_specs=[pl.BlockSpec((rows, bk, LANES), lambda j: (0, j, 0)),
                  pl.BlockSpec((bk, n), lambda j: (j, 0)),
                  pl.BlockSpec((1, n), lambda j: (0, 0))],
        out_specs=pl.BlockSpec((SUBLANES, n), lambda j: (0, 0)),
        out_shape=jax.ShapeDtypeStruct((SUBLANES, n), F32),
        scratch_shapes=[pltpu.VMEM((rows, SUBLANES, n), F32)],
        compiler_params=pltpu.CompilerParams(dimension_semantics=("arbitrary",),
                                             vmem_limit_bytes=VMEM_LIMIT_BYTES),
        name="adaln",
    )(cb, w, b)


def _halo(xv, nbr, reverse, sub):
    p = xv.shape[0]
    out = []
    for j in range(SHORT - 1):
        if reverse:
            own = pltpu.roll(xv[j], SUBLANES - 1, axis=0)
            oth = pltpu.roll(nbr[j], SUBLANES - 1, axis=0)
            out.append(jnp.where(sub == SUBLANES - 1, oth, own))
        else:
            own = pltpu.roll(xv[p - (SHORT - 1) + j], 1, axis=0)
            oth = pltpu.roll(nbr[j], 1, axis=0)
            out.append(jnp.where(sub == 0, oth, own))
    return jnp.stack(out, axis=0)


def _short_conv(xv, halo, cw_ref, cb, reverse):
    p = xv.shape[0]
    xe = jnp.concatenate([xv, halo] if reverse else [halo, xv], axis=0)
    acc = cb[None] + cw_ref[0][None] * xe[0:p]
    for k in range(1, SHORT):
        acc = acc + cw_ref[k][None] * xe[k:k + p]
    return acc


def _gates_ab(xc, wri_ref, br, bi, lam, a_ref, b_ref):
    p, _, c = xc.shape
    x2 = xc.reshape(p * SUBLANES, c)
    xb = x2.astype(BF16)
    neg = -lam
    softplus = jnp.maximum(neg, 0.0) + jnp.log(1.0 + jnp.exp(-jnp.abs(neg)))
    cl2 = softplus * (-LRU_C * math.log2(math.e))
    for h in range(c // RNN_BW):
        sl = slice(h * RNN_BW, (h + 1) * RNN_BW)
        pre = jnp.dot(xb[:, sl], wri_ref[h], preferred_element_type=F32)
        r = _sigmoid(pre[:, :RNN_BW] + br[:, sl])
        i = _sigmoid(pre[:, RNN_BW:] + bi[:, sl])
        a = jnp.exp2(r * cl2[:, sl])
        om = 1.0 - a * a
        mult = jnp.where(om > 0.0, om * lax.rsqrt(om), 0.0)
        b = mult * (i * x2[:, sl])
        a_ref[:, :, sl] = a.reshape(p, SUBLANES, RNN_BW)
        b_ref[:, :, sl] = b.reshape(p, SUBLANES, RNN_BW)


def _lru_scan(a_ref, b_ref, carry_prev, reverse, h_ref):
    p, _, c = a_ref.shape
    sub = lax.broadcasted_iota(jnp.int32, (SUBLANES, c), 0)
    order = range(p - 1, -1, -1) if reverse else range(p)
    b_end = jnp.zeros((SUBLANES, c), F32)
    a_end = jnp.ones((SUBLANES, c), F32)
    for t in order:
        a = a_ref[t]
        b_end = a * b_end + b_ref[t]
        a_end = a_end * a
    for s in (1, 2, 4):
        shift = (SUBLANES - s) if reverse else s
        a_sh = pltpu.roll(a_end, shift, axis=0)
        b_sh = pltpu.roll(b_end, shift, axis=0)
        valid = (sub < SUBLANES - s) if reverse else (sub >= s)
        b_end = jnp.where(valid, a_end * b_sh + b_end, b_end)
        a_end = jnp.where(valid, a_end * a_sh, a_end)
    last = 0 if reverse else SUBLANES - 1
    first = SUBLANES - 1 - last
    c_in = jnp.broadcast_to(carry_prev[last:last + 1, :], (SUBLANES, c))
    h_out = b_end + a_end * c_in
    if h_ref is not None:
        h = jnp.where(sub == first, c_in,
                      pltpu.roll(h_out, (SUBLANES - 1) if reverse else 1, axis=0))
        for t in order:
            h = a_ref[t] * h + b_ref[t]
            h_ref[t] = h
    return h_out


def _rnn_gates(xr_ref, groups, nbr_ref, cw_ref, cb, wri_ref, br, bi, lam, a_s, b_s, reverse):
    gp, _, c = xr_ref.shape
    p = gp // groups
    sub = lax.broadcasted_iota(jnp.int32, (SUBLANES, c), 0)
    xcs = [None] * groups
    for g in (range(groups - 1, -1, -1) if reverse else range(groups)):
        xv = xr_ref[g * p:(g + 1) * p]
        halo = _halo(xv, nbr_ref[...], reverse, sub)
        nbr_ref[...] = xv[0:SHORT - 1] if reverse else xv[p - (SHORT - 1):p]
        xcs[g] = _short_conv(xv, halo, cw_ref, cb, reverse)
    xc = xcs[0] if groups == 1 else jnp.concatenate(xcs, axis=0)
    _gates_ab(xc, wri_ref, br, bi, lam, a_s, b_s)


def _rnn_scan(carry_ref, a_s, b_s, reverse, h_ref):
    h_out = _lru_scan(a_s, b_s, carry_ref[...], reverse, h_ref)
    carry_ref[...] = h_out
    return h_out


def _rnn_tiles_kernel(x_ref, shift_ref, scale_ref, g_ref, w_ref, bx_ref, cw_ref, cb_ref, wri_ref,
                      br_ref, bi_ref, lam_ref, h0_ref, *rest, reverse, mod_row, write_states, c, n_cast):
    cast_in, rest = rest[:n_cast], rest[n_cast:]
    if write_states:
        xr_ref, h_ref, hn_s = rest[:3]
        cast_out = rest[3:3 + n_cast]
        a_s, b_s, nbr_s, carry_s = rest[3 + n_cast:]
        hfin_ref = None
        for src, dst in zip(cast_in, cast_out):
            dst[...] = src[...].astype(BF16)
    else:
        hfin_ref, hn_s, xr_ref, a_s, b_s, nbr_s, carry_s = rest
        h_ref = None
    i = pl.program_id(0)
    p, _, d = x_ref.shape
    dr = w_ref.shape[1]
    nc = dr // c
    nb = c // RNN_BW

    @pl.when(i == 0)
    def _():
        nbr_s[...] = jnp.zeros(nbr_s.shape, F32)
        carry_s[...] = h0_ref[...]

    def lanes(k):
        return slice(k * c, (k + 1) * c)

    def project(k):
        xr = jnp.dot(hn_s[...], w_ref[:, lanes(k)], preferred_element_type=F32) + bx_ref[:, lanes(k)]
        xr_ref[:, :, lanes(k)] = xr.reshape(p, SUBLANES, c)

    def gates(k):
        sl = lanes(k)
        _rnn_gates(xr_ref.at[:, :, sl], 1, nbr_s.at[:, :, sl], cw_ref.at[:, :, sl], cb_ref[:, sl],
                   wri_ref.at[k * nb:(k + 1) * nb], br_ref[:, sl], bi_ref[:, sl], lam_ref[:, sl],
                   a_s, b_s, reverse)

    def scan(k):
        sl = lanes(k)
        h_end = _rnn_scan(carry_s.at[:, sl], a_s, b_s, reverse,
                          None if h_ref is None else h_ref.at[:, :, sl])
        if hfin_ref is not None:
            hfin_ref[:, sl] = h_end

    x = x_ref[...].reshape(p * SUBLANES, d)
    row = slice(mod_row, mod_row + 1)
    hn_s[...] = _rms_mod(x, g_ref[...], scale_ref[row, :], shift_ref[row, :]).astype(BF16)
    project(0)
    for k in range(nc):

        @pl.when(_always(i, k))
        def _(k=k):
            gates(k)
            if k + 1 < nc:
                project(k + 1)
            scan(k)


def _rnn_tiles(x3, mod, norm_g, w_in, b_in, xr_col, cw, cb, wri, br, bi, lam, h0, *, p, c, reverse,
               mod_row, write_states, to_bf16=()):
    s8, _, d = x3.shape
    dr = cw.shape[-1]
    nt = s8 // p
    t = p * SUBLANES
    assert write_states or not to_bf16

    def tile(i):
        return (nt - 1 - i) if reverse else i

    def whole(shape):
        return pl.BlockSpec(shape, lambda i: (0,) * len(shape))

    in_specs = [
        pl.BlockSpec((p, SUBLANES, d), lambda i: (tile(i), 0, 0)),
        pl.BlockSpec((SUBLANES, d), lambda i: (0, 0)),
        pl.BlockSpec((SUBLANES, d), lambda i: (0, 1)),
        whole((1, d)),
        pl.BlockSpec((d, dr), lambda i: (0, xr_col), pipeline_mode=pl.Buffered(1)),
        pl.BlockSpec((1, dr), lambda i: (0, xr_col)),
        whole((SHORT, SUBLANES, dr)), whole((SUBLANES, dr)),
        whole((dr // RNN_BW, RNN_BW, 2 * RNN_BW)),
        whole((1, dr)), whole((1, dr)), whole((1, dr)),
        whole((SUBLANES, dr)),
    ]
    state = jax.ShapeDtypeStruct((s8, SUBLANES, dr), F32)
    cast_specs = [pl.BlockSpec((m.shape[0] // nt, m.shape[1]), lambda i: (i, 0)) for m in to_bf16]
    assert all(m.shape[0] % (2 * SUBLANES * nt) == 0 for m in to_bf16)
    in_specs += cast_specs
    if write_states:
        st_spec = pl.BlockSpec((p, SUBLANES, dr), lambda i: (tile(i), 0, 0))
        out_specs = [st_spec, st_spec, pl.BlockSpec((t, d), lambda i: (tile(i), 0))] + cast_specs
        out_shape = ([state, state, jax.ShapeDtypeStruct((s8 * SUBLANES, d), BF16)]
                     + [jax.ShapeDtypeStruct(m.shape, BF16) for m in to_bf16])
        own_scratch = []
    else:
        assert nt == 1
        out_specs = whole((SUBLANES, dr))
        out_shape = jax.ShapeDtypeStruct((SUBLANES, dr), F32)
        own_scratch = [pltpu.VMEM((t, d), BF16), pltpu.VMEM((p, SUBLANES, dr), F32)]
    scratch = (own_scratch + [
        pltpu.VMEM((p, SUBLANES, c), F32),
        pltpu.VMEM((p, SUBLANES, c), F32),
        pltpu.VMEM((SHORT - 1, SUBLANES, dr), F32),
        pltpu.VMEM((SUBLANES, dr), F32),
    ])
    kern = functools.partial(_rnn_tiles_kernel, reverse=reverse, mod_row=mod_row,
                             write_states=write_states, c=c, n_cast=len(to_bf16))
    return pl.pallas_call(
        kern,
        grid=(nt,),
        in_specs=in_specs,
        out_specs=out_specs,
        out_shape=out_shape,
        scratch_shapes=scratch,
        compiler_params=pltpu.CompilerParams(dimension_semantics=("arbitrary",),
                                             vmem_limit_bytes=VMEM_LIMIT_BYTES),
        name="rnn_rev" if reverse else "rnn_fwd",
    )(x3, mod, mod, norm_g, w_in, b_in, cw, cb, wri, br, bi, lam, h0, *to_bf16)


def _mixer_kernel(hn_ref,
                  wa_ref, wg_ref, wzc_ref, wzr_ref, wgc_ref, wgr_ref,
                  ba_ref, bg_ref, bzc_ref, bzr_ref, bgc_ref, bgr_ref,
                  dw_ref, dwb_ref,
                  xr_ref, hb_ref, cw_ref, cb_ref, wri_ref, br_ref, bi_ref, lam_ref, h0_ref,
                  cv_ref, szc_ref, lr_ref, sgc_ref, sgr_ref,
                  hn_s, vpad_s, a_s, b_s, hf_s, nbr_s, carry_s, *, p):
    i = pl.program_id(0)
    j = pl.program_id(1)
    gp = xr_ref.shape[0]
    groups = gp // p
    c = wa_ref.shape[1]
    t = gp * SUBLANES

    @pl.when(j == 0)
    def _():
        hn_s[...] = hn_ref[...]

    @pl.when((i == 0) & (j == 0))
    def _():
        zeros = jnp.zeros((CONV_PAD, SUBLANES, c), F32)
        for g in range(groups):
            vpad_s[g, 0:CONV_PAD] = zeros
            vpad_s[g, CONV_PAD + p:CONV_PAD + p + CONV_PAD] = zeros

    @pl.when(i == 0)
    def _():
        nbr_s[j] = jnp.zeros(nbr_s.shape[1:], F32)
        carry_s[j] = h0_ref[...]

    def proj(w_ref, b_ref):
        return jnp.dot(hn_s[...], w_ref[...], preferred_element_type=F32) + b_ref[...]

    _rnn_gates(xr_ref, groups, nbr_s.at[j], cw_ref, cb_ref[...], wri_ref, br_ref[...], bi_ref[...],
               lam_ref[...], a_s, b_s, False)
    v = (proj(wa_ref, ba_ref) * _sigmoid(proj(wg_ref, bg_ref))).reshape(gp, SUBLANES, c)
    for g in range(groups):
        vpad_s[g, CONV_PAD:CONV_PAD + p] = v[g * p:(g + 1) * p]

    @pl.when(_always(i, 0))
    def _():
        off = CONV_PAD - CONV_HALF
        pb = CONV_BLOCK
        for g in range(groups):
            for q in range(p // pb):
                for lb in range(c // LANES):
                    sl = slice(lb * LANES, (lb + 1) * LANES)
                    acc = jnp.broadcast_to(dwb_ref[:, sl][None], (pb, SUBLANES, LANES))
                    for k in range(CONV_W):
                        lo_k = q * pb + k + off
                        acc = acc + dw_ref[k, :, sl][None] * vpad_s[g, lo_k:lo_k + pb, :, sl]
                    cv_ref[g * p + q * pb:g * p + (q + 1) * pb, :, sl] = acc
        szc_ref[...] = _silu(proj(wzc_ref, bzc_ref)).astype(BF16)
        for g in range(groups):
            rows = slice(g * p, (g + 1) * p)
            _rnn_scan(carry_s.at[j], a_s.at[rows], b_s.at[rows], False, hf_s.at[rows])
        szr = _silu(proj(wzr_ref, bzr_ref))
        h = (hf_s[...] + hb_ref[...]).reshape(t, c)
        lr_ref[...] = (h * szr).astype(BF16)
        sgc_ref[...] = _sigmoid(proj(wgc_ref, bgc_ref)).astype(BF16)
        sgr_ref[...] = _sigmoid(proj(wgr_ref, bgr_ref)).astype(BF16)


def _mixer(hn, w_in, b_in, dw, dwb, xr, hb, cw, cb, wri, br, bi, lam, h0, *, p, groups, c):
    s, d = hn.shape
    s8 = s // SUBLANES
    gp = groups * p
    nt = s8 // gp
    nc = d // c
    t = gp * SUBLANES
    assert nt * gp == s8 and nc * c == d

    def wspec(piece):
        return pl.BlockSpec((d, c), lambda i, j: (0, piece * nc + j))

    def bspec(piece):
        return pl.BlockSpec((1, c), lambda i, j: (0, piece * nc + j))

    pieces = (0, 1, 2, 4, 5, 6)
    chunk_row = pl.BlockSpec((1, c), lambda i, j: (0, j))
    chunk_8 = pl.BlockSpec((SUBLANES, c), lambda i, j: (0, j))
    tok3 = pl.BlockSpec((gp, SUBLANES, c), lambda i, j: (i, 0, j))
    tok2 = pl.BlockSpec((t, c), lambda i, j: (i, j))
    in_specs = (
        [pl.BlockSpec((t, d), lambda i, j: (i, 0))]
        + [wspec(k) for k in pieces]
        + [bspec(k) for k in pieces]
        + [pl.BlockSpec((CONV_W, SUBLANES, c), lambda i, j: (0, 0, j)), chunk_row,
           tok3, tok3,
           pl.BlockSpec((SHORT, SUBLANES, c), lambda i, j: (0, 0, j)), chunk_8,
           pl.BlockSpec((c // RNN_BW, RNN_BW, 2 * RNN_BW), lambda i, j: (j, 0, 0)),
           chunk_row, chunk_row, chunk_row, chunk_8])
    out_specs = [tok3, tok2, tok2, tok2, tok2]
    out_shape = [jax.ShapeDtypeStruct((s8, SUBLANES, d), F32)] + [jax.ShapeDtypeStruct((s, d), BF16)] * 4
    scratch = [
        pltpu.VMEM((t, d), BF16),
        pltpu.VMEM((groups, p + 2 * CONV_PAD, SUBLANES, c), F32),
        pltpu.VMEM((gp, SUBLANES, c), F32),
        pltpu.VMEM((gp, SUBLANES, c), F32),
        pltpu.VMEM((gp, SUBLANES, c), F32),
        pltpu.VMEM((nc, SHORT - 1, SUBLANES, c), F32),
        pltpu.VMEM((nc, SUBLANES, c), F32),
    ]
    args = ([hn] + [w_in] * len(pieces) + [b_in] * len(pieces)
            + [dw, dwb, xr, hb, cw, cb, wri, br, bi, lam, h0])
    return pl.pallas_call(
        functools.partial(_mixer_kernel, p=p),
        grid=(nt, nc),
        in_specs=in_specs,
        out_specs=out_specs,
        out_shape=out_shape,
        scratch_shapes=scratch,
        compiler_params=pltpu.CompilerParams(dimension_semantics=("arbitrary", "arbitrary"),
                                             vmem_limit_bytes=VMEM_LIMIT_BYTES),
        name="mixer",
    )(*args)


def _out_kernel(cv_ref, szc_ref, lr_ref, sgc_ref, sgr_ref, x_ref, gate_ref, lng_ref, lnb_ref, fg_ref,
                wc_ref, wr_ref, wo_ref, o_ref, res_s):
    yr = jnp.dot(lr_ref[...], wr_ref[...], preferred_element_type=F32)
    cv = cv_ref[...]
    mu = jnp.mean(cv, axis=-1, keepdims=True)
    dev = cv - mu
    var = jnp.mean(dev * dev, axis=-1, keepdims=True)
    y = dev * lax.rsqrt(var + EPS) * lng_ref[...] + lnb_ref[...]
    lc = (_silu(y) * szc_ref[...].astype(F32)).astype(BF16)
    yc = jnp.dot(lc, wc_ref[...], preferred_element_type=F32)
    y = (sgc_ref[...].astype(F32) * yc + sgr_ref[...].astype(F32) * yr).astype(BF16)
    o = jnp.dot(y, wo_ref[...], preferred_element_type=F32)
    xn = x_ref[...] + gate_ref[0:1, :] * o
    ms = jnp.mean(xn * xn, axis=-1, keepdims=True)
    res = xn * lax.rsqrt(ms + EPS) * fg_ref[...]
    t, d = res.shape
    for lb in range(d // LANES):
        res_s[lb] = res[:, lb * LANES:(lb + 1) * LANES]
    for g in range(SUBLANES):
        for lb in range(d // LANES):
            o_ref[g, :, lb * LANES:(lb + 1) * LANES] = res_s[lb, pl.ds(g, t // SUBLANES, stride=SUBLANES), :]


def _out_proj(cv, szc, lr, sgc, sgr, x2, mod, ln_g, ln_b, final_g, wc, wr, wo, *, t, p):
    s, d = x2.shape
    per_group = (p * SUBLANES) // t
    assert per_group * t == p * SUBLANES and t % (SUBLANES * SUBLANES) == 0
    tok = pl.BlockSpec((t, d), lambda i: (i, 0))
    row = pl.BlockSpec((1, d), lambda i: (0, 0))
    resident = pl.BlockSpec((d, d), lambda i: (0, 0), pipeline_mode=pl.Buffered(1))
    return pl.pallas_call(
        _out_kernel,
        grid=(s // t,),
        in_specs=[tok, tok, tok, tok, tok, tok,
                  pl.BlockSpec((SUBLANES, d), lambda i: (0, 2)),
                  row, row, row,
                  resident, resident, resident],
        out_specs=pl.BlockSpec((None, SUBLANES, t // SUBLANES, d),
                               lambda i: (i // per_group, 0, i % per_group, 0)),
        out_shape=jax.ShapeDtypeStruct((s // (p * SUBLANES), SUBLANES, p, d), F32),
        scratch_shapes=[pltpu.VMEM((d // LANES, t, LANES), F32)],
        compiler_params=pltpu.CompilerParams(dimension_semantics=("arbitrary",),
                                             vmem_limit_bytes=VMEM_LIMIT_BYTES),
        name="out_proj",
    )(cv, szc, lr, sgc, sgr, x2, mod, ln_g, ln_b, final_g, wc, wr, wo)


def _permute_tokens(x2, p):
    s, d = x2.shape
    nt = s // (p * SUBLANES)
    return x2.reshape(nt, SUBLANES, p, d).transpose(0, 2, 1, 3).reshape(s // SUBLANES, SUBLANES, d)


def kernel(x, c, ctx, c_ctx, w_ada, b_ada, norm_g, w_in, b_in, conv_dw, conv_dw_b, conv_ln_g, conv_ln_b, w_conv_out, rnn_conv, rnn_conv_b, rnn_w_r, rnn_b_r, rnn_w_i, rnn_b_i, rnn_lam, w_rnn_out, w_o, final_g):
    bsz, seq, d = x.shape
    ctx_len = ctx.shape[1]
    depth = w_in.shape[0]
    assert bsz == 1 and depth == 1
    assert w_in.shape[2] == 7 * d and rnn_w_r.shape[3] == RNN_BW and conv_dw.shape[1] == CONV_W
    assert seq % (SUBLANES * GRID_W) == 0 and ctx_len % SUBLANES == 0 and d % LANES == 0
    l = 0
    chunk = min(d, 256)
    p_ctx = ctx_len // SUBLANES

    w_xr = w_in[l][:, 3 * d:4 * d].astype(BF16)
    b_in2 = b_in[l][None]
    b_xr = b_in2[:, 3 * d:4 * d]
    wri = jnp.concatenate([rnn_w_r[l], rnn_w_i[l]], axis=-1).astype(BF16)
    cw = jnp.broadcast_to(rnn_conv[l][:, :, None, :], (2, SHORT, SUBLANES, d))
    cb = jnp.broadcast_to(rnn_conv_b[l][:, None, :], (2, SUBLANES, d))
    br = rnn_b_r[l][:, None, :]
    bi = rnn_b_i[l][:, None, :]
    lam = rnn_lam[l][:, None, :]
    dw = jnp.broadcast_to(conv_dw[l][:, None, :], (CONV_W, SUBLANES, d))
    norm_g2 = norm_g[l][None]

    cond = jnp.stack([c[0], c_ctx])
    mod = _adaln(jnp.broadcast_to(cond[:, :, None], (2, d, LANES)), w_ada[l], b_ada[l][None])

    def rnn_args(direction):
        return (cw[direction], cb[direction], wri[direction], br[direction], bi[direction],
                lam[direction])

    ctx3 = _permute_tokens(ctx[0], p_ctx)
    zeros8 = jnp.zeros((SUBLANES, d), F32)
    h0 = [_rnn_tiles(ctx3, mod, norm_g2, w_xr, b_xr, 0, *rnn_args(direction), zeros8,
                     p=p_ctx, c=chunk, reverse=bool(direction), mod_row=1, write_states=False)
          for direction in (0, 1)]

    x3 = _permute_tokens(x[0], GRID_W)
    xr, hb, hn, w_in_b = _rnn_tiles(
        x3, mod, norm_g2, w_xr, b_xr, 0, *rnn_args(1), h0[1], p=GRID_W, c=min(d, 512), reverse=True,
        mod_row=0, write_states=True, to_bf16=(w_in[l],))
    groups = 2 if seq % (2 * SUBLANES * GRID_W) == 0 else 1
    cv, szc, lr, sgc, sgr = _mixer(hn, w_in_b, b_in2, dw, conv_dw_b[l][None], xr, hb,
                                   *rnn_args(0), h0[0], p=GRID_W, groups=groups, c=chunk)
    out = _out_proj(cv.reshape(seq, d), szc, lr, sgc, sgr, x3.reshape(seq, d), mod,
                    conv_ln_g[l][None], conv_ln_b[l][None], final_g[None],
                    w_conv_out[l].astype(BF16), w_rnn_out[l].astype(BF16), w_o[l].astype(BF16),
                    t=min(seq, 256), p=GRID_W)
    return out.reshape(1, seq, d)
```

```python
import functools
import math

import jax
import jax.numpy as jnp
from jax import lax
from jax.experimental import pallas as pl
from jax.experimental.pallas import tpu as pltpu

EPS = 1e-6
LRU_C = 8.0
GRID_W = 64
CONV_W = 31
CONV_HALF = CONV_W // 2
CONV_PAD = 16
CONV_BLOCK = 8
SHORT = 4
RNN_BW = 128

SUBLANES = 8
LANES = 128
VMEM_LIMIT_BYTES = 56 * 1024 * 1024

F32 = jnp.float32
BF16 = jnp.bfloat16


def _sigmoid(x):
    return jax.nn.sigmoid(x)


def _silu(x):
    return x * jax.nn.sigmoid(x)


def _always(i, k):
    return i + k >= 0


def _rms_mod(x, g, scale, shift):
    ms = jnp.mean(x * x, axis=-1, keepdims=True)
    y = x * lax.rsqrt(ms + EPS)
    return y * (g * (1.0 + scale)) + shift


def _adaln_kernel(cb_ref, w_ref, b_ref, o_ref, acc_s):
    j = pl.program_id(0)
    bk, n = w_ref.shape
    rows = cb_ref.shape[0]

    @pl.when(j == 0)
    def _():
        acc_s[...] = jnp.zeros(acc_s.shape, F32)

    for r in range(rows):
        s3 = _silu(cb_ref[r]).reshape(bk // SUBLANES, SUBLANES, LANES)
        for nb in range(n // LANES):
            sl = slice(nb * LANES, (nb + 1) * LANES)
            w3 = w_ref[:, sl].reshape(bk // SUBLANES, SUBLANES, LANES)
            acc_s[r, :, sl] += jnp.sum(w3 * s3, axis=0)

    @pl.when(j == pl.num_programs(0) - 1)
    def _():
        o_ref[...] = jnp.zeros(o_ref.shape, F32)
        for r in range(rows):
            o_ref[r:r + 1, :] = jnp.sum(acc_s[r], axis=0, keepdims=True) + b_ref[...]


def _adaln(cb, w, b):
    rows, d, _ = cb.shape
    n = w.shape[1]
    bk = min(d, 128)
    assert d % bk == 0 and rows <= SUBLANES
    return pl.pallas_call(
        _adaln_kernel,
        grid=(d // bk,),
        in_specs=[pl.BlockSpec((rows, bk, LANES), lambda j: (0, j, 0)),
                  pl.BlockSpec((bk, n), lambda j: (j, 0)),
                  pl.BlockSpec((1, n), lambda j: (0, 0))],
        out_specs=pl.BlockSpec((SUBLANES, n), lambda j: (0, 0)),
        out_shape=jax.ShapeDtypeStruct((SUBLANES, n), F32),
        scratch_shapes=[pltpu.VMEM((rows, SUBLANES, n), F32)],
        compiler_params=pltpu.CompilerParams(dimension_semantics=("arbitrary",),
                                             vmem_limit_bytes=VMEM_LIMIT_BYTES),
        name="adaln",
    )(cb, w, b)


def _halo(xv, nbr, reverse, sub):
    p = xv.shape[0]
    out = []
    for j in range(SHORT - 1):
        if reverse:
            own = pltpu.roll(xv[j], SUBLANES - 1, axis=0)
            oth = pltpu.roll(nbr[j], SUBLANES - 1, axis=0)
            out.append(jnp.where(sub == SUBLANES - 1, oth, own))
        else:
            own = pltpu.roll(xv[p - (SHORT - 1) + j], 1, axis=0)
            oth = pltpu.roll(nbr[j], 1, axis=0)
            out.append(jnp.where(sub == 0, oth, own))
    return jnp.stack(out, axis=0)


def _short_conv(xv, halo, cw_ref, cb, reverse):
    p = xv.shape[0]
    xe = jnp.concatenate([xv, halo] if reverse else [halo, xv], axis=0)
    acc = cb[None] + cw_ref[0][None] * xe[0:p]
    for k in range(1, SHORT):
        acc = acc + cw_ref[k][None] * xe[k:k + p]
    return acc


def _gates_ab(xc, wri_ref, br, bi, lam, a_ref, b_ref):
    p, _, c = xc.shape
    x2 = xc.reshape(p * SUBLANES, c)
    xb = x2.astype(BF16)
    neg = -lam
    softplus = jnp.maximum(neg, 0.0) + jnp.log(1.0 + jnp.exp(-jnp.abs(neg)))
    cl2 = softplus * (-LRU_C * math.log2(math.e))
    for h in range(c // RNN_BW):
        sl = slice(h * RNN_BW, (h + 1) * RNN_BW)
        pre = jnp.dot(xb[:, sl], wri_ref[h], preferred_element_type=F32)
        r = _sigmoid(pre[:, :RNN_BW] + br[:, sl])
        i = _sigmoid(pre[:, RNN_BW:] + bi[:, sl])
        a = jnp.exp2(r * cl2[:, sl])
        om = 1.0 - a * a
        mult = jnp.where(om > 0.0, om * lax.rsqrt(om), 0.0)
        b = mult * (i * x2[:, sl])
        a_ref[:, :, sl] = a.reshape(p, SUBLANES, RNN_BW)
        b_ref[:, :, sl] = b.reshape(p, SUBLANES, RNN_BW)


def _lru_scan(a_ref, b_ref, carry_prev, reverse, h_ref):
    p, _, c = a_ref.shape
    sub = lax.broadcasted_iota(jnp.int32, (SUBLANES, c), 0)
    order = range(p - 1, -1, -1) if reverse else range(p)
    b_end = jnp.zeros((SUBLANES, c), F32)
    a_end = jnp.ones((SUBLANES, c), F32)
    for t in order:
        a = a_ref[t]
        b_end = a * b_end + b_ref[t]
        a_end = a_end * a
    for s in (1, 2, 4):
        shift = (SUBLANES - s) if reverse else s
        a_sh = pltpu.roll(a_end, shift, axis=0)
        b_sh = pltpu.roll(b_end, shift, axis=0)
        valid = (sub < SUBLANES - s) if reverse else (sub >= s)
        b_end = jnp.where(valid, a_end * b_sh + b_end, b_end)
        a_end = jnp.where(valid, a_end * a_sh, a_end)
    last = 0 if reverse else SUBLANES - 1
    first = SUBLANES - 1 - last
    c_in = jnp.broadcast_to(carry_prev[last:last + 1, :], (SUBLANES, c))
    h_out = b_end + a_end * c_in
    if h_ref is not None:
        h = jnp.where(sub == first, c_in,
                      pltpu.roll(h_out, (SUBLANES - 1) if reverse else 1, axis=0))
        for t in order:
            h = a_ref[t] * h + b_ref[t]
            h_ref[t] = h
    return h_out


def _rnn_gates(xr_ref, groups, nbr_ref, cw_ref, cb, wri_ref, br, bi, lam, a_s, b_s, reverse):
    gp, _, c = xr_ref.shape
    p = gp // groups
    sub = lax.broadcasted_iota(jnp.int32, (SUBLANES, c), 0)
    xcs = [None] * groups
    for g in (range(groups - 1, -1, -1) if reverse else range(groups)):
        xv = xr_ref[g * p:(g + 1) * p]
        halo = _halo(xv, nbr_ref[...], reverse, sub)
        nbr_ref[...] = xv[0:SHORT - 1] if reverse else xv[p - (SHORT - 1):p]
        xcs[g] = _short_conv(xv, halo, cw_ref, cb, reverse)
    xc = xcs[0] if groups == 1 else jnp.concatenate(xcs, axis=0)
    _gates_ab(xc, wri_ref, br, bi, lam, a_s, b_s)


def _rnn_scan(carry_ref, a_s, b_s, reverse, h_ref):
    h_out = _lru_scan(a_s, b_s, carry_ref[...], reverse, h_ref)
    carry_ref[...] = h_out
    return h_out


def _rnn_tiles_kernel(x_ref, shift_ref, scale_ref, g_ref, w_ref, bx_ref, cw_ref, cb_ref, wri_ref,
                      br_ref, bi_ref, lam_ref, h0_ref, *rest, reverse, mod_row, write_states, c, n_cast):
    cast_in, rest = rest[:n_cast], rest[n_cast:]
    if write_states:
        xr_ref, h_ref, hn_s = rest[:3]
        cast_out = rest[3:3 + n_cast]
        a_s, b_s, nbr_s, carry_s = rest[3 + n_cast:]
        hfin_ref = None
        for src, dst in zip(cast_in, cast_out):
            dst[...] = src[...].astype(BF16)
    else:
        hfin_ref, hn_s, xr_ref, a_s, b_s, nbr_s, carry_s = rest
        h_ref = None
    i = pl.program_id(0)
    p, _, d = x_ref.shape
    dr = w_ref.shape[1]
    nc = dr // c
    nb = c // RNN_BW

    @pl.when(i == 0)
    def _():
        nbr_s[...] = jnp.zeros(nbr_s.shape, F32)
        carry_s[...] = h0_ref[...]

    def lanes(k):
        return slice(k * c, (k + 1) * c)

    def project(k):
        xr = jnp.dot(hn_s[...], w_ref[:, lanes(k)], preferred_element_type=F32) + bx_ref[:, lanes(k)]
        xr_ref[:, :, lanes(k)] = xr.reshape(p, SUBLANES, c)

    def gates(k):
        sl = lanes(k)
        _rnn_gates(xr_ref.at[:, :, sl], 1, nbr_s.at[:, :, sl], cw_ref.at[:, :, sl], cb_ref[:, sl],
                   wri_ref.at[k * nb:(k + 1) * nb], br_ref[:, sl], bi_ref[:, sl], lam_ref[:, sl],
                   a_s, b_s, reverse)

    def scan(k):
        sl = lanes(k)
        h_end = _rnn_scan(carry_s.at[:, sl], a_s, b_s, reverse,
                          None if h_ref is None else h_ref.at[:, :, sl])
        if hfin_ref is not None:
            hfin_ref[:, sl] = h_end

    x = x_ref[...].reshape(p * SUBLANES, d)
    row = slice(mod_row, mod_row + 1)
    hn_s[...] = _rms_mod(x, g_ref[...], scale_ref[row, :], shift_ref[row, :]).astype(BF16)
    project(0)
    for k in range(nc):

        @pl.when(_always(i, k))
        def _(k=k):
            gates(k)
            if k + 1 < nc:
                project(k + 1)
            scan(k)


def _rnn_tiles(x3, mod, norm_g, w_in, b_in, xr_col, cw, cb, wri, br, bi, lam, h0, *, p, c, reverse,
               mod_row, write_states, to_bf16=()):
    s8, _, d = x3.shape
    dr = cw.shape[-1]
    nt = s8 // p
    t = p * SUBLANES
    assert write_states or not to_bf16

    def tile(i):
        return (nt - 1 - i) if reverse else i

    def whole(shape):
        return pl.BlockSpec(shape, lambda i: (0,) * len(shape))

    in_specs = [
        pl.BlockSpec((p, SUBLANES, d), lambda i: (tile(i), 0, 0)),
        pl.BlockSpec((SUBLANES, d), lambda i: (0, 0)),
        pl.BlockSpec((SUBLANES, d), lambda i: (0, 1)),
        whole((1, d)),
        pl.BlockSpec((d, dr), lambda i: (0, xr_col), pipeline_mode=pl.Buffered(1)),
        pl.BlockSpec((1, dr), lambda i: (0, xr_col)),
        whole((SHORT, SUBLANES, dr)), whole((SUBLANES, dr)),
        whole((dr // RNN_BW, RNN_BW, 2 * RNN_BW)),
        whole((1, dr)), whole((1, dr)), whole((1, dr)),
        whole((SUBLANES, dr)),
    ]
    state = jax.ShapeDtypeStruct((s8, SUBLANES, dr), F32)
    cast_specs = [pl.BlockSpec((m.shape[0] // nt, m.shape[1]), lambda i: (i, 0)) for m in to_bf16]
    assert all(m.shape[0] % (2 * SUBLANES * nt) == 0 for m in to_bf16)
    in_specs += cast_specs
    if write_states:
        st_spec = pl.BlockSpec((p, SUBLANES, dr), lambda i: (tile(i), 0, 0))
        out_specs = [st_spec, st_spec, pl.BlockSpec((t, d), lambda i: (tile(i), 0))] + cast_specs
        out_shape = ([state, state, jax.ShapeDtypeStruct((s8 * SUBLANES, d), BF16)]
                     + [jax.ShapeDtypeStruct(m.shape, BF16) for m in to_bf16])
        own_scratch = []
    else:
        assert nt == 1
        out_specs = whole((SUBLANES, dr))
        out_shape = jax.ShapeDtypeStruct((SUBLANES, dr), F32)
        own_scratch = [pltpu.VMEM((t, d), BF16), pltpu.VMEM((p, SUBLANES, dr), F32)]
    scratch = (own_scratch + [
        pltpu.VMEM((p, SUBLANES, c), F32),
        pltpu.VMEM((p, SUBLANES, c), F32),
        pltpu.VMEM((SHORT - 1, SUBLANES, dr), F32),
        pltpu.VMEM((SUBLANES, dr), F32),
    ])
    kern = functools.partial(_rnn_tiles_kernel, reverse=reverse, mod_row=mod_row,
                             write_states=write_states, c=c, n_cast=len(to_bf16))
    return pl.pallas_call(
        kern,
        grid=(nt,),
        in_specs=in_specs,
        out_specs=out_specs,
        out_shape=out_shape,
        scratch_shapes=scratch,
        compiler_params=pltpu.CompilerParams(dimension_semantics=("arbitrary",),
                                             vmem_limit_bytes=VMEM_LIMIT_BYTES),
        name="rnn_rev" if reverse else "rnn_fwd",
    )(x3, mod, mod, norm_g, w_in, b_in, cw, cb, wri, br, bi, lam, h0, *to_bf16)


def _mixer_kernel(hn_ref,
                  wa_ref, wg_ref, wzc_ref, wzr_ref, wgc_ref, wgr_ref,
                  ba_ref, bg_ref, bzc_ref, bzr_ref, bgc_ref, bgr_ref,
                  dw_ref, dwb_ref,
                  xr_ref, hb_ref, cw_ref, cb_ref, wri_ref, br_ref, bi_ref, lam_ref, h0_ref,
                  *rest, p, n_cast):
    cast_in, rest = rest[:n_cast], rest[n_cast:]
    cv_ref, szc_ref, lr_ref, sgc_ref, sgr_ref = rest[:5]
    cast_out = rest[5:5 + n_cast]
    hn_s, vpad_s, a_s, b_s, hf_s, nbr_s, carry_s = rest[5 + n_cast:]
    for src, dst in zip(cast_in, cast_out):
        dst[...] = src[...].astype(BF16)
    i = pl.program_id(0)
    j = pl.program_id(1)
    gp = xr_ref.shape[0]
    groups = gp // p
    c = wa_ref.shape[1]
    t = gp * SUBLANES

    @pl.when(j == 0)
    def _():
        hn_s[...] = hn_ref[...]

    @pl.when((i == 0) & (j == 0))
    def _():
        zeros = jnp.zeros((CONV_PAD, SUBLANES, c), F32)
        for g in range(groups):
            vpad_s[g, 0:CONV_PAD] = zeros
            vpad_s[g, CONV_PAD + p:CONV_PAD + p + CONV_PAD] = zeros

    @pl.when(i == 0)
    def _():
        nbr_s[j] = jnp.zeros(nbr_s.shape[1:], F32)
        carry_s[j] = h0_ref[...]

    def proj(w_ref, b_ref):
        return jnp.dot(hn_s[...], w_ref[...], preferred_element_type=F32) + b_ref[...]

    _rnn_gates(xr_ref, groups, nbr_s.at[j], cw_ref, cb_ref[...], wri_ref, br_ref[...], bi_ref[...],
               lam_ref[...], a_s, b_s, False)
    v = (proj(wa_ref, ba_ref) * _sigmoid(proj(wg_ref, bg_ref))).reshape(gp, SUBLANES, c)
    for g in range(groups):
        vpad_s[g, CONV_PAD:CONV_PAD + p] = v[g * p:(g + 1) * p]

    @pl.when(_always(i, 0))
    def _():
        off = CONV_PAD - CONV_HALF
        pb = CONV_BLOCK
        for g in range(groups):
            for q in range(p // pb):
                for lb in range(c // LANES):
                    sl = slice(lb * LANES, (lb + 1) * LANES)
                    acc = jnp.broadcast_to(dwb_ref[:, sl][None], (pb, SUBLANES, LANES))
                    for k in range(CONV_W):
                        lo_k = q * pb + k + off
                        acc = acc + dw_ref[k, :, sl][None] * vpad_s[g, lo_k:lo_k + pb, :, sl]
                    cv_ref[g * p + q * pb:g * p + (q + 1) * pb, :, sl] = acc
        szc_ref[...] = _silu(proj(wzc_ref, bzc_ref)).astype(BF16)
        for g in range(groups):
            rows = slice(g * p, (g + 1) * p)
            _rnn_scan(carry_s.at[j], a_s.at[rows], b_s.at[rows], False, hf_s.at[rows])
        szr = _silu(proj(wzr_ref, bzr_ref))
        h = (hf_s[...] + hb_ref[...]).reshape(t, c)
        lr_ref[...] = (h * szr).astype(BF16)
        sgc_ref[...] = _sigmoid(proj(wgc_ref, bgc_ref)).astype(BF16)
        sgr_ref[...] = _sigmoid(proj(wgr_ref, bgr_ref)).astype(BF16)


def _mixer(hn, w_in, b_in, dw, dwb, xr, hb, cw, cb, wri, br, bi, lam, h0, *, p, groups, c, to_bf16=()):
    s, d = hn.shape
    s8 = s // SUBLANES
    gp = groups * p
    nt = s8 // gp
    nc = d // c
    t = gp * SUBLANES
    assert nt * gp == s8 and nc * c == d

    def wspec(piece):
        return pl.BlockSpec((d, c), lambda i, j: (0, piece * nc + j))

    def bspec(piece):
        return pl.BlockSpec((1, c), lambda i, j: (0, piece * nc + j))

    pieces = (0, 1, 2, 4, 5, 6)
    chunk_row = pl.BlockSpec((1, c), lambda i, j: (0, j))
    chunk_8 = pl.BlockSpec((SUBLANES, c), lambda i, j: (0, j))
    tok3 = pl.BlockSpec((gp, SUBLANES, c), lambda i, j: (i, 0, j))
    tok2 = pl.BlockSpec((t, c), lambda i, j: (i, j))
    in_specs = (
        [pl.BlockSpec((t, d), lambda i, j: (i, 0))]
        + [wspec(k) for k in pieces]
        + [bspec(k) for k in pieces]
        + [pl.BlockSpec((CONV_W, SUBLANES, c), lambda i, j: (0, 0, j)), chunk_row,
           tok3, tok3,
           pl.BlockSpec((SHORT, SUBLANES, c), lambda i, j: (0, 0, j)), chunk_8,
           pl.BlockSpec((c // RNN_BW, RNN_BW, 2 * RNN_BW), lambda i, j: (j, 0, 0)),
           chunk_row, chunk_row, chunk_row, chunk_8])
    steps = nt * nc
    assert all(m.shape[0] % (2 * SUBLANES * steps) == 0 for m in to_bf16)
    cast_specs = [pl.BlockSpec((m.shape[0] // steps, m.shape[1]), lambda i, j: (i * nc + j, 0))
                  for m in to_bf16]
    in_specs = in_specs + cast_specs
    out_specs = [tok3, tok2, tok2, tok2, tok2] + cast_specs
    out_shape = ([jax.ShapeDtypeStruct((s8, SUBLANES, d), F32)] + [jax.ShapeDtypeStruct((s, d), BF16)] * 4
                 + [jax.ShapeDtypeStruct(m.shape, BF16) for m in to_bf16])
    scratch = [
        pltpu.VMEM((t, d), BF16),
        pltpu.VMEM((groups, p + 2 * CONV_PAD, SUBLANES, c), F32),
        pltpu.VMEM((gp, SUBLANES, c), F32),
        pltpu.VMEM((gp, SUBLANES, c), F32),
        pltpu.VMEM((gp, SUBLANES, c), F32),
        pltpu.VMEM((nc, SHORT - 1, SUBLANES, c), F32),
        pltpu.VMEM((nc, SUBLANES, c), F32),
    ]
    args = ([hn] + [w_in] * len(pieces) + [b_in] * len(pieces)
            + [dw, dwb, xr, hb, cw, cb, wri, br, bi, lam, h0] + list(to_bf16))
    return pl.pallas_call(
        functools.partial(_mixer_kernel, p=p, n_cast=len(to_bf16)),
        grid=(nt, nc),
        in_specs=in_specs,
        out_specs=out_specs,
        out_shape=out_shape,
        scratch_shapes=scratch,
        compiler_params=pltpu.CompilerParams(dimension_semantics=("arbitrary", "arbitrary"),
                                             vmem_limit_bytes=VMEM_LIMIT_BYTES),
        name="mixer",
    )(*args)


def _out_kernel(cv_ref, szc_ref, lr_ref, sgc_ref, sgr_ref, x_ref, gate_ref, lng_ref, lnb_ref, fg_ref,
                wc_ref, wr_ref, wo_ref, o_ref, res_s):
    yr = jnp.dot(lr_ref[...], wr_ref[...], preferred_element_type=F32)
    cv = cv_ref[...]
    mu = jnp.mean(cv, axis=-1, keepdims=True)
    dev = cv - mu
    var = jnp.mean(dev * dev, axis=-1, keepdims=True)
    y = dev * lax.rsqrt(var + EPS) * lng_ref[...] + lnb_ref[...]
    lc = (_silu(y) * szc_ref[...].astype(F32)).astype(BF16)
    yc = jnp.dot(lc, wc_ref[...], preferred_element_type=F32)
    y = (sgc_ref[...].astype(F32) * yc + sgr_ref[...].astype(F32) * yr).astype(BF16)
    o = jnp.dot(y, wo_ref[...], preferred_element_type=F32)
    xn = x_ref[...] + gate_ref[0:1, :] * o
    ms = jnp.mean(xn * xn, axis=-1, keepdims=True)
    res = xn * lax.rsqrt(ms + EPS) * fg_ref[...]
    t, d = res.shape
    for lb in range(d // LANES):
        res_s[lb] = res[:, lb * LANES:(lb + 1) * LANES]
    for g in range(SUBLANES):
        for lb in range(d // LANES):
            o_ref[g, :, lb * LANES:(lb + 1) * LANES] = res_s[lb, pl.ds(g, t // SUBLANES, stride=SUBLANES), :]


def _out_proj(cv, szc, lr, sgc, sgr, x2, mod, ln_g, ln_b, final_g, wc, wr, wo, *, t, p):
    s, d = x2.shape
    per_group = (p * SUBLANES) // t
    assert per_group * t == p * SUBLANES and t % (SUBLANES * SUBLANES) == 0
    tok = pl.BlockSpec((t, d), lambda i: (i, 0))
    row = pl.BlockSpec((1, d), lambda i: (0, 0))
    resident = pl.BlockSpec((d, d), lambda i: (0, 0), pipeline_mode=pl.Buffered(1))
    return pl.pallas_call(
        _out_kernel,
        grid=(s // t,),
        in_specs=[tok, tok, tok, tok, tok, tok,
                  pl.BlockSpec((SUBLANES, d), lambda i: (0, 2)),
                  row, row, row,
                  resident, resident, resident],
        out_specs=pl.BlockSpec((None, SUBLANES, t // SUBLANES, d),
                               lambda i: (i // per_group, 0, i % per_group, 0)),
        out_shape=jax.ShapeDtypeStruct((s // (p * SUBLANES), SUBLANES, p, d), F32),
        scratch_shapes=[pltpu.VMEM((d // LANES, t, LANES), F32)],
        compiler_params=pltpu.CompilerParams(dimension_semantics=("arbitrary",),
                                             vmem_limit_bytes=VMEM_LIMIT_BYTES),
        name="out_proj",
    )(cv, szc, lr, sgc, sgr, x2, mod, ln_g, ln_b, final_g, wc, wr, wo)


def _permute_tokens(x2, p):
    s, d = x2.shape
    nt = s // (p * SUBLANES)
    return x2.reshape(nt, SUBLANES, p, d).transpose(0, 2, 1, 3).reshape(s // SUBLANES, SUBLANES, d)


def kernel(x, c, ctx, c_ctx, w_ada, b_ada, norm_g, w_in, b_in, conv_dw, conv_dw_b, conv_ln_g, conv_ln_b, w_conv_out, rnn_conv, rnn_conv_b, rnn_w_r, rnn_b_r, rnn_w_i, rnn_b_i, rnn_lam, w_rnn_out, w_o, final_g):
    bsz, seq, d = x.shape
    ctx_len = ctx.shape[1]
    depth = w_in.shape[0]
    assert bsz == 1 and depth == 1
    assert w_in.shape[2] == 7 * d and rnn_w_r.shape[3] == RNN_BW and conv_dw.shape[1] == CONV_W
    assert seq % (SUBLANES * GRID_W) == 0 and ctx_len % SUBLANES == 0 and d % LANES == 0
    l = 0
    chunk = min(d, 256)
    p_ctx = ctx_len // SUBLANES

    w_xr = w_in[l][:, 3 * d:4 * d].astype(BF16)
    b_in2 = b_in[l][None]
    b_xr = b_in2[:, 3 * d:4 * d]
    wri = jnp.concatenate([rnn_w_r[l], rnn_w_i[l]], axis=-1).astype(BF16)
    cw = jnp.broadcast_to(rnn_conv[l][:, :, None, :], (2, SHORT, SUBLANES, d))
    cb = jnp.broadcast_to(rnn_conv_b[l][:, None, :], (2, SUBLANES, d))
    br = rnn_b_r[l][:, None, :]
    bi = rnn_b_i[l][:, None, :]
    lam = rnn_lam[l][:, None, :]
    dw = jnp.broadcast_to(conv_dw[l][:, None, :], (CONV_W, SUBLANES, d))
    norm_g2 = norm_g[l][None]

    cond = jnp.stack([c[0], c_ctx])
    mod = _adaln(jnp.broadcast_to(cond[:, :, None], (2, d, LANES)), w_ada[l], b_ada[l][None])

    def rnn_args(direction):
        return (cw[direction], cb[direction], wri[direction], br[direction], bi[direction],
                lam[direction])

    ctx3 = _permute_tokens(ctx[0], p_ctx)
    zeros8 = jnp.zeros((SUBLANES, d), F32)
    h0 = [_rnn_tiles(ctx3, mod, norm_g2, w_xr, b_xr, 0, *rnn_args(direction), zeros8,
                     p=p_ctx, c=chunk, reverse=bool(direction), mod_row=1, write_states=False)
          for direction in (0, 1)]

    x3 = _permute_tokens(x[0], GRID_W)
    xr, hb, hn, w_in_b = _rnn_tiles(
        x3, mod, norm_g2, w_xr, b_xr, 0, *rnn_args(1), h0[1], p=GRID_W, c=min(d, 512), reverse=True,
        mod_row=0, write_states=True, to_bf16=(w_in[l],))
    groups = 2 if seq % (2 * SUBLANES * GRID_W) == 0 else 1
    n_steps = (seq // (groups * SUBLANES * GRID_W)) * (d // chunk)
    out_w = (w_conv_out[l], w_rnn_out[l], w_o[l])
    in_mixer = d % (2 * SUBLANES * n_steps) == 0
    cv, szc, lr, sgc, sgr, *out_w_b = _mixer(hn, w_in_b, b_in2, dw, conv_dw_b[l][None], xr, hb,
                                             *rnn_args(0), h0[0], p=GRID_W, groups=groups, c=chunk,
                                             to_bf16=out_w if in_mixer else ())
    if not in_mixer:
        out_w_b = [w.astype(BF16) for w in out_w]
    out = _out_proj(cv.reshape(seq, d), szc, lr, sgc, sgr, x3.reshape(seq, d), mod,
                    conv_ln_g[l][None], conv_ln_b[l][None], final_g[None], *out_w_b,
                    t=min(seq, 256), p=GRID_W)
    return out.reshape(1, seq, d)
```

```python
import functools
import math

import jax
import jax.numpy as jnp
from jax import lax
from jax.experimental import pallas as pl
from jax.experimental.pallas import tpu as pltpu

EPS = 1e-6
LRU_C = 8.0
GRID_W = 64
CONV_W = 31
CONV_HALF = CONV_W // 2
CONV_PAD = 16
CONV_BLOCK = 8
SHORT = 4
RNN_BW = 128

SUBLANES = 8
LANES = 128
VMEM_LIMIT_BYTES = 56 * 1024 * 1024

F32 = jnp.float32
BF16 = jnp.bfloat16


def _sigmoid(x):
    return jax.nn.sigmoid(x)


def _silu(x):
    return x * jax.nn.sigmoid(x)


def _always(i, k):
    return i + k >= 0


def _rms_mod(x, g, scale, shift):
    ms = jnp.mean(x * x, axis=-1, keepdims=True)
    y = x * lax.rsqrt(ms + EPS)
    return y * (g * (1.0 + scale)) + shift


def _adaln_kernel(cb_ref, w_ref, b_ref, o_ref, acc_s):
    j = pl.program_id(0)
    bk, n = w_ref.shape
    rows = cb_ref.shape[0]

    @pl.when(j == 0)
    def _():
        acc_s[...] = jnp.zeros(acc_s.shape, F32)

    for r in range(rows):
        s3 = _silu(cb_ref[r]).reshape(bk // SUBLANES, SUBLANES, LANES)
        for nb in range(n // LANES):
            sl = slice(nb * LANES, (nb + 1) * LANES)
            w3 = w_ref[:, sl].reshape(bk // SUBLANES, SUBLANES, LANES)
            acc_s[r, :, sl] += jnp.sum(w3 * s3, axis=0)

    @pl.when(j == pl.num_programs(0) - 1)
    def _():
        o_ref[...] = jnp.zeros(o_ref.shape, F32)
        for r in range(rows):
            o_ref[r:r + 1, :] = jnp.sum(acc_s[r], axis=0, keepdims=True) + b_ref[...]


def _adaln(cb, w, b):
    rows, d, _ = cb.shape
    n = w.shape[1]
    bk = min(d, 128)
    assert d % bk == 0 and rows <= SUBLANES
    return pl.pallas_call(
        _adaln_kernel,
        grid=(d // bk,),
        in_specs=[pl.BlockSpec((rows, bk, LANES), lambda j: (0, j, 0)),
                  pl.BlockSpec((bk, n), lambda j: (j, 0)),
                  pl.BlockSpec((1, n), lambda j: (0, 0))],
        out_specs=pl.BlockSpec((SUBLANES, n), lambda j: (0, 0)),
        out_shape=jax.ShapeDtypeStruct((SUBLANES, n), F32),
        scratch_shapes=[pltpu.VMEM((rows, SUBLANES, n), F32)],
        compiler_params=pltpu.CompilerParams(dimension_semantics=("arbitrary",),
                                             vmem_limit_bytes=VMEM_LIMIT_BYTES),
        name="adaln",
    )(cb, w, b)


def _halo(xv, nbr, reverse, sub):
    p = xv.shape[0]
    out = []
    for j in range(SHORT - 1):
        if reverse:
            own = pltpu.roll(xv[j], SUBLANES - 1, axis=0)
            oth = pltpu.roll(nbr[j], SUBLANES - 1, axis=0)
            out.append(jnp.where(sub == SUBLANES - 1, oth, own))
        else:
            own = pltpu.roll(xv[p - (SHORT - 1) + j], 1, axis=0)
            oth = pltpu.roll(nbr[j], 1, axis=0)
            out.append(jnp.where(sub == 0, oth, own))
    return jnp.stack(out, axis=0)


def _short_conv(xv, halo, cw_ref, cb, reverse):
    p = xv.shape[0]
    xe = jnp.concatenate([xv, halo] if reverse else [halo, xv], axis=0)
    acc = cb[None] + cw_ref[0][None] * xe[0:p]
    for k in range(1, SHORT):
        acc = acc + cw_ref[k][None] * xe[k:k + p]
    return acc


def _gates_ab(xc, wri_ref, br, bi, lam, a_ref, b_ref):
    p, _, c = xc.shape
    x2 = xc.reshape(p * SUBLANES, c)
    xb = x2.astype(BF16)
    neg = -lam
    softplus = jnp.maximum(neg, 0.0) + jnp.log(1.0 + jnp.exp(-jnp.abs(neg)))
    cl2 = softplus * (-LRU_C * math.log2(math.e))
    for h in range(c // RNN_BW):
        sl = slice(h * RNN_BW, (h + 1) * RNN_BW)
        pre = jnp.dot(xb[:, sl], wri_ref[h], preferred_element_type=F32)
        r = _sigmoid(pre[:, :RNN_BW] + br[:, sl])
        i = _sigmoid(pre[:, RNN_BW:] + bi[:, sl])
        a = jnp.exp2(r * cl2[:, sl])
        om = 1.0 - a * a
        mult = jnp.where(om > 0.0, om * lax.rsqrt(om), 0.0)
        b = mult * (i * x2[:, sl])
        a_ref[:, :, sl] = a.reshape(p, SUBLANES, RNN_BW)
        b_ref[:, :, sl] = b.reshape(p, SUBLANES, RNN_BW)


def _lru_scan(a_ref, b_ref, carry_prev, reverse, h_ref):
    p, _, c = a_ref.shape
    sub = lax.broadcasted_iota(jnp.int32, (SUBLANES, c), 0)
    order = range(p - 1, -1, -1) if reverse else range(p)
    b_end = jnp.zeros((SUBLANES, c), F32)
    a_end = jnp.ones((SUBLANES, c), F32)
    for t in order:
        a = a_ref[t]
        b_end = a * b_end + b_ref[t]
        a_end = a_end * a
    for s in (1, 2, 4):
        shift = (SUBLANES - s) if reverse else s
        a_sh = pltpu.roll(a_end, shift, axis=0)
        b_sh = pltpu.roll(b_end, shift, axis=0)
        valid = (sub < SUBLANES - s) if reverse else (sub >= s)
        b_end = jnp.where(valid, a_end * b_sh + b_end, b_end)
        a_end = jnp.where(valid, a_end * a_sh, a_end)
    last = 0 if reverse else SUBLANES - 1
    first = SUBLANES - 1 - last
    c_in = jnp.broadcast_to(carry_prev[last:last + 1, :], (SUBLANES, c))
    h_out = b_end + a_end * c_in
    if h_ref is not None:
        h = jnp.where(sub == first, c_in,
                      pltpu.roll(h_out, (SUBLANES - 1) if reverse else 1, axis=0))
        for t in order:
            h = a_ref[t] * h + b_ref[t]
            h_ref[t] = h
    return h_out


def _rnn_gates(xr_ref, groups, nbr_ref, cw_ref, cb, wri_ref, br, bi, lam, a_s, b_s, reverse):
    gp, _, c = xr_ref.shape
    p = gp // groups
    sub = lax.broadcasted_iota(jnp.int32, (SUBLANES, c), 0)
    xcs = [None] * groups
    for g in (range(groups - 1, -1, -1) if reverse else range(groups)):
        xv = xr_ref[g * p:(g + 1) * p]
        halo = _halo(xv, nbr_ref[...], reverse, sub)
        nbr_ref[...] = xv[0:SHORT - 1] if reverse else xv[p - (SHORT - 1):p]
        xcs[g] = _short_conv(xv, halo, cw_ref, cb, reverse)
    xc = xcs[0] if groups == 1 else jnp.concatenate(xcs, axis=0)
    _gates_ab(xc, wri_ref, br, bi, lam, a_s, b_s)


def _rnn_scan(carry_ref, a_s, b_s, reverse, h_ref):
    h_out = _lru_scan(a_s, b_s, carry_ref[...], reverse, h_ref)
    carry_ref[...] = h_out
    return h_out


def _rnn_tiles_kernel(x_ref, shift_ref, scale_ref, g_ref, w_ref, bx_ref, cw_ref, cb_ref, wri_ref,
                      br_ref, bi_ref, lam_ref, h0_ref, *rest, reverse, mod_row, write_states, c, n_cast,
                      emit_w=False):
    cast_in, rest = rest[:n_cast], rest[n_cast:]
    if write_states:
        xr_ref, h_ref, hn_s = rest[:3]
        cast_out = rest[3:3 + n_cast]
        a_s, b_s, nbr_s, carry_s = rest[3 + n_cast:]
        hfin_ref = None
        for src, dst in zip(cast_in, cast_out):
            dst[...] = src[...].astype(BF16)
    else:
        hfin_ref, rest = rest[0], rest[1:]
        wout_ref, rest = (rest[0], rest[1:]) if emit_w else (None, rest)
        hn_s, xr_ref, a_s, b_s, nbr_s, carry_s = rest
        h_ref = None
    i = pl.program_id(0)
    p, _, d = x_ref.shape
    dr = w_ref.shape[1]
    nc = dr // c
    nb = c // RNN_BW

    @pl.when(i == 0)
    def _():
        nbr_s[...] = jnp.zeros(nbr_s.shape, F32)
        carry_s[...] = h0_ref[...]

    def lanes(k):
        return slice(k * c, (k + 1) * c)

    def project(k):
        w = w_ref[:, lanes(k)]
        if emit_w:
            w = w.astype(BF16)
            wout_ref[:, lanes(k)] = w
        xr = jnp.dot(hn_s[...], w, preferred_element_type=F32) + bx_ref[:, lanes(k)]
        xr_ref[:, :, lanes(k)] = xr.reshape(p, SUBLANES, c)

    def gates(k):
        sl = lanes(k)
        _rnn_gates(xr_ref.at[:, :, sl], 1, nbr_s.at[:, :, sl], cw_ref.at[:, :, sl], cb_ref[:, sl],
                   wri_ref.at[k * nb:(k + 1) * nb], br_ref[:, sl], bi_ref[:, sl], lam_ref[:, sl],
                   a_s, b_s, reverse)

    def scan(k):
        sl = lanes(k)
        h_end = _rnn_scan(carry_s.at[:, sl], a_s, b_s, reverse,
                          None if h_ref is None else h_ref.at[:, :, sl])
        if hfin_ref is not None:
            hfin_ref[:, sl] = h_end

    x = x_ref[...].reshape(p * SUBLANES, d)
    row = slice(mod_row, mod_row + 1)
    hn_s[...] = _rms_mod(x, g_ref[...], scale_ref[row, :], shift_ref[row, :]).astype(BF16)
    project(0)
    for k in range(nc):

        @pl.when(_always(i, k))
        def _(k=k):
            gates(k)
            if k + 1 < nc:
                project(k + 1)
            scan(k)


def _rnn_tiles(x3, mod, norm_g, w_in, b_in, xr_col, cw, cb, wri, br, bi, lam, h0, *, p, c, reverse,
               mod_row, write_states, to_bf16=(), emit_w=False):
    s8, _, d = x3.shape
    dr = cw.shape[-1]
    nt = s8 // p
    t = p * SUBLANES
    assert write_states or not to_bf16

    def tile(i):
        return (nt - 1 - i) if reverse else i

    def whole(shape):
        return pl.BlockSpec(shape, lambda i: (0,) * len(shape))

    in_specs = [
        pl.BlockSpec((p, SUBLANES, d), lambda i: (tile(i), 0, 0)),
        pl.BlockSpec((SUBLANES, d), lambda i: (0, 0)),
        pl.BlockSpec((SUBLANES, d), lambda i: (0, 1)),
        whole((1, d)),
        pl.BlockSpec((d, dr), lambda i: (0, xr_col), pipeline_mode=pl.Buffered(1)),
        pl.BlockSpec((1, dr), lambda i: (0, xr_col)),
        whole((SHORT, SUBLANES, dr)), whole((SUBLANES, dr)),
        whole((dr // RNN_BW, RNN_BW, 2 * RNN_BW)),
        whole((1, dr)), whole((1, dr)), whole((1, dr)),
        whole((SUBLANES, dr)),
    ]
    state = jax.ShapeDtypeStruct((s8, SUBLANES, dr), F32)
    cast_specs = [pl.BlockSpec((m.shape[0] // nt, m.shape[1]), lambda i: (i, 0)) for m in to_bf16]
    assert all(m.shape[0] % (2 * SUBLANES * nt) == 0 for m in to_bf16)
    in_specs += cast_specs
    if write_states:
        st_spec = pl.BlockSpec((p, SUBLANES, dr), lambda i: (tile(i), 0, 0))
        out_specs = [st_spec, st_spec, pl.BlockSpec((t, d), lambda i: (tile(i), 0))] + cast_specs
        out_shape = ([state, state, jax.ShapeDtypeStruct((s8 * SUBLANES, d), BF16)]
                     + [jax.ShapeDtypeStruct(m.shape, BF16) for m in to_bf16])
        own_scratch = []
    else:
        assert nt == 1
        out_specs = whole((SUBLANES, dr))
        out_shape = jax.ShapeDtypeStruct((SUBLANES, dr), F32)
        if emit_w:
            out_specs = [out_specs, whole((d, dr))]
            out_shape = [out_shape, jax.ShapeDtypeStruct((d, dr), BF16)]
        own_scratch = [pltpu.VMEM((t, d), BF16), pltpu.VMEM((p, SUBLANES, dr), F32)]
    scratch = (own_scratch + [
        pltpu.VMEM((p, SUBLANES, c), F32),
        pltpu.VMEM((p, SUBLANES, c), F32),
        pltpu.VMEM((SHORT - 1, SUBLANES, dr), F32),
        pltpu.VMEM((SUBLANES, dr), F32),
    ])
    kern = functools.partial(_rnn_tiles_kernel, reverse=reverse, mod_row=mod_row,
                             write_states=write_states, c=c, n_cast=len(to_bf16), emit_w=emit_w)
    return pl.pallas_call(
        kern,
        grid=(nt,),
        in_specs=in_specs,
        out_specs=out_specs,
        out_shape=out_shape,
        scratch_shapes=scratch,
        compiler_params=pltpu.CompilerParams(dimension_semantics=("arbitrary",),
                                             vmem_limit_bytes=VMEM_LIMIT_BYTES),
        name="rnn_rev" if reverse else "rnn_fwd",
    )(x3, mod, mod, norm_g, w_in, b_in, cw, cb, wri, br, bi, lam, h0, *to_bf16)


def _mixer_kernel(hn_ref,
                  wa_ref, wg_ref, wzc_ref, wzr_ref, wgc_ref, wgr_ref,
                  ba_ref, bg_ref, bzc_ref, bzr_ref, bgc_ref, bgr_ref,
                  dw_ref, dwb_ref,
                  xr_ref, hb_ref, cw_ref, cb_ref, wri_ref, br_ref, bi_ref, lam_ref, h0_ref,
                  *rest, p, n_cast):
    cast_in, rest = rest[:n_cast], rest[n_cast:]
    cv_ref, szc_ref, lr_ref, sgc_ref, sgr_ref = rest[:5]
    cast_out = rest[5:5 + n_cast]
    hn_s, vpad_s, a_s, b_s, hf_s, nbr_s, carry_s = rest[5 + n_cast:]
    for src, dst in zip(cast_in, cast_out):
        dst[...] = src[...].astype(BF16)
    i = pl.program_id(0)
    j = pl.program_id(1)
    gp = xr_ref.shape[0]
    groups = gp // p
    c = wa_ref.shape[1]
    t = gp * SUBLANES

    @pl.when(j == 0)
    def _():
        hn_s[...] = hn_ref[...]

    @pl.when((i == 0) & (j == 0))
    def _():
        zeros = jnp.zeros((CONV_PAD, SUBLANES, c), F32)
        for g in range(groups):
            vpad_s[g, 0:CONV_PAD] = zeros
            vpad_s[g, CONV_PAD + p:CONV_PAD + p + CONV_PAD] = zeros

    @pl.when(i == 0)
    def _():
        nbr_s[j] = jnp.zeros(nbr_s.shape[1:], F32)
        carry_s[j] = h0_ref[...]

    def proj(w_ref, b_ref):
        return jnp.dot(hn_s[...], w_ref[...], preferred_element_type=F32) + b_ref[...]

    _rnn_gates(xr_ref, groups, nbr_s.at[j], cw_ref, cb_ref[...], wri_ref, br_ref[...], bi_ref[...],
               lam_ref[...], a_s, b_s, False)
    v = (proj(wa_ref, ba_ref) * _sigmoid(proj(wg_ref, bg_ref))).reshape(gp, SUBLANES, c)
    for g in range(groups):
        vpad_s[g, CONV_PAD:CONV_PAD + p] = v[g * p:(g + 1) * p]

    @pl.when(_always(i, 0))
    def _():
        off = CONV_PAD - CONV_HALF
        pb = CONV_BLOCK
        for g in range(groups):
            for q in range(p // pb):
                for lb in range(c // LANES):
                    sl = slice(lb * LANES, (lb + 1) * LANES)
                    acc = jnp.broadcast_to(dwb_ref[:, sl][None], (pb, SUBLANES, LANES))
                    for k in range(CONV_W):
                        lo_k = q * pb + k + off
                        acc = acc + dw_ref[k, :, sl][None] * vpad_s[g, lo_k:lo_k + pb, :, sl]
                    cv_ref[g * p + q * pb:g * p + (q + 1) * pb, :, sl] = acc
        szc_ref[...] = _silu(proj(wzc_ref, bzc_ref)).astype(BF16)
        for g in range(groups):
            rows = slice(g * p, (g + 1) * p)
            _rnn_scan(carry_s.at[j], a_s.at[rows], b_s.at[rows], False, hf_s.at[rows])
        szr = _silu(proj(wzr_ref, bzr_ref))
        h = (hf_s[...] + hb_ref[...]).reshape(t, c)
        lr_ref[...] = (h * szr).astype(BF16)
        sgc_ref[...] = _sigmoid(proj(wgc_ref, bgc_ref)).astype(BF16)
        sgr_ref[...] = _sigmoid(proj(wgr_ref, bgr_ref)).astype(BF16)


def _mixer(hn, w_in, b_in, dw, dwb, xr, hb, cw, cb, wri, br, bi, lam, h0, *, p, groups, c, to_bf16=()):
    s, d = hn.shape
    s8 = s // SUBLANES
    gp = groups * p
    nt = s8 // gp
    nc = d // c
    t = gp * SUBLANES
    assert nt * gp == s8 and nc * c == d

    def wspec(piece):
        return pl.BlockSpec((d, c), lambda i, j: (0, piece * nc + j))

    def bspec(piece):
        return pl.BlockSpec((1, c), lambda i, j: (0, piece * nc + j))

    pieces = (0, 1, 2, 4, 5, 6)
    chunk_row = pl.BlockSpec((1, c), lambda i, j: (0, j))
    chunk_8 = pl.BlockSpec((SUBLANES, c), lambda i, j: (0, j))
    tok3 = pl.BlockSpec((gp, SUBLANES, c), lambda i, j: (i, 0, j))
    tok2 = pl.BlockSpec((t, c), lambda i, j: (i, j))
    in_specs = (
        [pl.BlockSpec((t, d), lambda i, j: (i, 0))]
        + [wspec(k) for k in pieces]
        + [bspec(k) for k in pieces]
        + [pl.BlockSpec((CONV_W, SUBLANES, c), lambda i, j: (0, 0, j)), chunk_row,
           tok3, tok3,
           pl.BlockSpec((SHORT, SUBLANES, c), lambda i, j: (0, 0, j)), chunk_8,
           pl.BlockSpec((c // RNN_BW, RNN_BW, 2 * RNN_BW), lambda i, j: (j, 0, 0)),
           chunk_row, chunk_row, chunk_row, chunk_8])
    steps = nt * nc
    assert all(m.shape[0] % (2 * SUBLANES * steps) == 0 for m in to_bf16)
    cast_specs = [pl.BlockSpec((m.shape[0] // steps, m.shape[1]), lambda i, j: (i * nc + j, 0))
                  for m in to_bf16]
    in_specs = in_specs + cast_specs
    out_specs = [tok3, tok2, tok2, tok2, tok2] + cast_specs
    out_shape = ([jax.ShapeDtypeStruct((s8, SUBLANES, d), F32)] + [jax.ShapeDtypeStruct((s, d), BF16)] * 4
                 + [jax.ShapeDtypeStruct(m.shape, BF16) for m in to_bf16])
    scratch = [
        pltpu.VMEM((t, d), BF16),
        pltpu.VMEM((groups, p + 2 * CONV_PAD, SUBLANES, c), F32),
        pltpu.VMEM((gp, SUBLANES, c), F32),
        pltpu.VMEM((gp, SUBLANES, c), F32),
        pltpu.VMEM((gp, SUBLANES, c), F32),
        pltpu.VMEM((nc, SHORT - 1, SUBLANES, c), F32),
        pltpu.VMEM((nc, SUBLANES, c), F32),
    ]
    args = ([hn] + [w_in] * len(pieces) + [b_in] * len(pieces)
            + [dw, dwb, xr, hb, cw, cb, wri, br, bi, lam, h0] + list(to_bf16))
    return pl.pallas_call(
        functools.partial(_mixer_kernel, p=p, n_cast=len(to_bf16)),
        grid=(nt, nc),
        in_specs=in_specs,
        out_specs=out_specs,
        out_shape=out_shape,
        scratch_shapes=scratch,
        compiler_params=pltpu.CompilerParams(dimension_semantics=("arbitrary", "arbitrary"),
                                             vmem_limit_bytes=VMEM_LIMIT_BYTES),
        name="mixer",
    )(*args)


def _out_kernel(cv_ref, szc_ref, lr_ref, sgc_ref, sgr_ref, x_ref, gate_ref, lng_ref, lnb_ref, fg_ref,
                wc_ref, wr_ref, wo_ref, o_ref, res_s):
    yr = jnp.dot(lr_ref[...], wr_ref[...], preferred_element_type=F32)
    cv = cv_ref[...]
    mu = jnp.mean(cv, axis=-1, keepdims=True)
    dev = cv - mu
    var = jnp.mean(dev * dev, axis=-1, keepdims=True)
    y = dev * lax.rsqrt(var + EPS) * lng_ref[...] + lnb_ref[...]
    lc = (_silu(y) * szc_ref[...].astype(F32)).astype(BF16)
    yc = jnp.dot(lc, wc_ref[...], preferred_element_type=F32)
    y = (sgc_ref[...].astype(F32) * yc + sgr_ref[...].astype(F32) * yr).astype(BF16)
    o = jnp.dot(y, wo_ref[...], preferred_element_type=F32)
    xn = x_ref[...] + gate_ref[0:1, :] * o
    ms = jnp.mean(xn * xn, axis=-1, keepdims=True)
    res = xn * lax.rsqrt(ms + EPS) * fg_ref[...]
    t, d = res.shape
    for lb in range(d // LANES):
        res_s[lb] = res[:, lb * LANES:(lb + 1) * LANES]
    for g in range(SUBLANES):
        for lb in range(d // LANES):
            o_ref[g, :, lb * LANES:(lb + 1) * LANES] = res_s[lb, pl.ds(g, t // SUBLANES, stride=SUBLANES), :]


def _out_proj(cv, szc, lr, sgc, sgr, x2, mod, ln_g, ln_b, final_g, wc, wr, wo, *, t, p):
    s, d = x2.shape
    per_group = (p * SUBLANES) // t
    assert per_group * t == p * SUBLANES and t % (SUBLANES * SUBLANES) == 0
    tok = pl.BlockSpec((t, d), lambda i: (i, 0))
    row = pl.BlockSpec((1, d), lambda i: (0, 0))
    resident = pl.BlockSpec((d, d), lambda i: (0, 0), pipeline_mode=pl.Buffered(1))
    return pl.pallas_call(
        _out_kernel,
        grid=(s // t,),
        in_specs=[tok, tok, tok, tok, tok, tok,
                  pl.BlockSpec((SUBLANES, d), lambda i: (0, 2)),
                  row, row, row,
                  resident, resident, resident],
        out_specs=pl.BlockSpec((None, SUBLANES, t // SUBLANES, d),
                               lambda i: (i // per_group, 0, i % per_group, 0)),
        out_shape=jax.ShapeDtypeStruct((s // (p * SUBLANES), SUBLANES, p, d), F32),
        scratch_shapes=[pltpu.VMEM((d // LANES, t, LANES), F32)],
        compiler_params=pltpu.CompilerParams(dimension_semantics=("arbitrary",),
                                             vmem_limit_bytes=VMEM_LIMIT_BYTES),
        name="out_proj",
    )(cv, szc, lr, sgc, sgr, x2, mod, ln_g, ln_b, final_g, wc, wr, wo)


def _permute_tokens(x2, p):
    s, d = x2.shape
    nt = s // (p * SUBLANES)
    return x2.reshape(nt, SUBLANES, p, d).transpose(0, 2, 1, 3).reshape(s // SUBLANES, SUBLANES, d)


def kernel(x, c, ctx, c_ctx, w_ada, b_ada, norm_g, w_in, b_in, conv_dw, conv_dw_b, conv_ln_g, conv_ln_b, w_conv_out, rnn_conv, rnn_conv_b, rnn_w_r, rnn_b_r, rnn_w_i, rnn_b_i, rnn_lam, w_rnn_out, w_o, final_g):
    bsz, seq, d = x.shape
    ctx_len = ctx.shape[1]
    depth = w_in.shape[0]
    assert bsz == 1 and depth == 1
    assert w_in.shape[2] == 7 * d and rnn_w_r.shape[3] == RNN_BW and conv_dw.shape[1] == CONV_W
    assert seq % (SUBLANES * GRID_W) == 0 and ctx_len % SUBLANES == 0 and d % LANES == 0
    l = 0
    chunk = min(d, 256)
    p_ctx = ctx_len // SUBLANES

    b_in2 = b_in[l][None]
    b_xr = b_in2[:, 3 * d:4 * d]
    wri = jnp.concatenate([rnn_w_r[l], rnn_w_i[l]], axis=-1).astype(BF16)
    cw = jnp.broadcast_to(rnn_conv[l][:, :, None, :], (2, SHORT, SUBLANES, d))
    cb = jnp.broadcast_to(rnn_conv_b[l][:, None, :], (2, SUBLANES, d))
    br = rnn_b_r[l][:, None, :]
    bi = rnn_b_i[l][:, None, :]
    lam = rnn_lam[l][:, None, :]
    dw = jnp.broadcast_to(conv_dw[l][:, None, :], (CONV_W, SUBLANES, d))
    norm_g2 = norm_g[l][None]

    cond = jnp.stack([c[0], c_ctx])
    mod = _adaln(jnp.broadcast_to(cond[:, :, None], (2, d, LANES)), w_ada[l], b_ada[l][None])

    def rnn_args(direction):
        return (cw[direction], cb[direction], wri[direction], br[direction], bi[direction],
                lam[direction])

    ctx3 = _permute_tokens(ctx[0], p_ctx)
    zeros8 = jnp.zeros((SUBLANES, d), F32)
    h0_fwd, w_xr = _rnn_tiles(ctx3, mod, norm_g2, w_in[l], b_in2, 3, *rnn_args(0), zeros8, p=p_ctx,
                              c=chunk, reverse=False, mod_row=1, write_states=False, emit_w=True)
    h0_rev = _rnn_tiles(ctx3, mod, norm_g2, w_xr, b_xr, 0, *rnn_args(1), zeros8, p=p_ctx,
                        c=chunk, reverse=True, mod_row=1, write_states=False)
    h0 = [h0_fwd, h0_rev]

    x3 = _permute_tokens(x[0], GRID_W)
    xr, hb, hn, w_in_b = _rnn_tiles(
        x3, mod, norm_g2, w_xr, b_xr, 0, *rnn_args(1), h0[1], p=GRID_W, c=min(d, 512), reverse=True,
        mod_row=0, write_states=True, to_bf16=(w_in[l],))
    groups = 2 if seq % (2 * SUBLANES * GRID_W) == 0 else 1
    n_steps = (seq // (groups * SUBLANES * GRID_W)) * (d // chunk)
    out_w = (w_conv_out[l], w_rnn_out[l], w_o[l])
    in_mixer = d % (2 * SUBLANES * n_steps) == 0
    cv, szc, lr, sgc, sgr, *out_w_b = _mixer(hn, w_in_b, b_in2, dw, conv_dw_b[l][None], xr, hb,
                                             *rnn_args(0), h0[0], p=GRID_W, groups=groups, c=chunk,
                                             to_bf16=out_w if in_mixer else ())
    if not in_mixer:
        out_w_b = [w.astype(BF16) for w in out_w]
    out = _out_proj(cv.reshape(seq, d), szc, lr, sgc, sgr, x3.reshape(seq, d), mod,
                    conv_ln_g[l][None], conv_ln_b[l][None], final_g[None], *out_w_b,
                    t=min(seq, 256), p=GRID_W)
    return out.reshape(1, seq, d)
```
